```python
import math
import jax, jax.numpy as jnp
from jax import lax
import numpy as np

D_MODEL = 4096
BATCH = 32
SEQ = 256
DEPTH = 2
DEC_BATCH = 8
DEC_SEQ = 4096
PAST_LEN = 512

GRID_W = 64
EPS = 1e-6
N_EVEN = (DEPTH + 1) // 2
N_ODD = DEPTH // 2
D_FF = 4 * D_MODEL
S5_WIDTH = D_MODEL // 2
S5_GROUP = 16
S5_GROUPS = S5_WIDTH // S5_GROUP
S5_STATE = 64
S5_CHUNK = 128
MLA_V_DIM = 128
MLA_HEADS = (D_MODEL - S5_WIDTH) // MLA_V_DIM
MLA_NOPE = 128
MLA_ROPE = 64
MLA_Q_RANK = 1024
MLA_KV_RANK = 512
ROPE_BASE = 10000.0
Q_BLOCK = 128
AB_IN = S5_WIDTH + MLA_Q_RANK + MLA_KV_RANK + MLA_ROPE
AB_OUT = S5_WIDTH + MLA_HEADS * MLA_V_DIM
HG_K = 128
HG_HEADS = D_MODEL // HG_K
HG_V = D_MODEL // HG_HEADS
HG_KW = HG_HEADS * HG_K
HG_VW = HG_HEADS * HG_V
HG_CHUNK = 32
HG_IN = 3 * HG_KW + 2 * HG_VW

kernel_name = 'hybrid_s5_mla_hgrn2_denoise_step'

F32 = jnp.float32


def rms_norm(x, g):
    x32 = x.astype(F32)
    y = x32 * lax.rsqrt(jnp.mean(x32 * x32, axis=-1, keepdims=True) + EPS)
    return (y * g.astype(F32)).astype(x.dtype)


def modulation(cond, w, b):
    m = jax.nn.silu(cond) @ w + b
    return jnp.split(m[:, None, :], 6, axis=-1)


def sq_relu_mlp(h, w1, w2):
    z = jax.nn.relu(h @ w1)
    return (z * z) @ w2


def axial_rope(x):
    n_l = x.shape[1]
    rows = n_l // GRID_W
    row = jnp.repeat(jnp.arange(rows), GRID_W)
    col = jnp.tile(jnp.arange(GRID_W), rows)
    half = MLA_ROPE // 2
    inv = ROPE_BASE ** (-jnp.arange(0, half, 2, dtype=F32) / half)
    bshape = (n_l,) + (1,) * (x.ndim - 3) + (half // 2,)
    xf = x.astype(F32)

    def rot(xa, pos):
        ang = (pos.astype(F32)[:, None] * inv[None]).reshape(bshape)
        cs, sn = jnp.cos(ang), jnp.sin(ang)
        x1, x2 = jnp.split(xa, 2, axis=-1)
        return jnp.concatenate([x1 * cs - x2 * sn, x1 * sn + x2 * cs], axis=-1)

    out = jnp.concatenate([rot(xf[..., :half], row), rot(xf[..., half:], col)], axis=-1)
    return out.astype(x.dtype)


def s5_discretize(log_dt, lam_re, lam_im, b_re, b_im):
    dt = jnp.exp(log_dt.astype(F32))[..., None]
    lr, li = lam_re.astype(F32), lam_im.astype(F32)
    mag = jnp.exp(lr * dt)
    ar, ai = mag * jnp.cos(li * dt), mag * jnp.sin(li * dt)
    den = lr * lr + li * li
    cr = ((ar - 1.0) * lr + ai * li) / den
    ci = (ai * lr - (ar - 1.0) * li) / den
    br, bi = b_re.astype(F32), b_im.astype(F32)
    bbr = cr[..., None] * br - ci[..., None] * bi
    bbi = cr[..., None] * bi + ci[..., None] * br
    return ar, ai, bbr, bbi


def _s5_combine(e1, e2):
    a1r, a1i, b1r, b1i = e1
    a2r, a2i, b2r, b2i = e2
    return (a2r * a1r - a2i * a1i, a2r * a1i + a2i * a1r,
            a2r * b1r - a2i * b1i + b2r, a2r * b1i + a2i * b1r + b2i)


def s5_scan(u, ar, ai, br, bi, cr, ci, h0r, h0i):
    n_b, n_l = u.shape[0], u.shape[1]
    n_c = n_l // S5_CHUNK
    uc = u.reshape(n_b, n_c, S5_CHUNK, S5_GROUPS, S5_GROUP).transpose(1, 2, 0, 3, 4)
    a_r = jnp.broadcast_to(ar[None, None], (S5_CHUNK, n_b, S5_GROUPS, S5_STATE))
    a_i = jnp.broadcast_to(ai[None, None], (S5_CHUNK, n_b, S5_GROUPS, S5_STATE))

    def body(carry, u_t):
        hr, hi = carry
        bur = jnp.einsum('tbgc,gpc->tbgp', u_t, br)
        bui = jnp.einsum('tbgc,gpc->tbgp', u_t, bi)
        pr, pi, sr, si = lax.associative_scan(_s5_combine, (a_r, a_i, bur, bui), axis=0)
        hr_t = pr * hr[None] - pi * hi[None] + sr
        hi_t = pr * hi[None] + pi * hr[None] + si
        y = jnp.einsum('tbgp,gcp->tbgc', hr_t, cr) - jnp.einsum('tbgp,gcp->tbgc', hi_t, ci)
        return (hr_t[-1], hi_t[-1]), y

    (hr, hi), ys = lax.scan(body, (h0r, h0i), uc)
    y = ys.transpose(2, 0, 1, 3, 4).reshape(n_b, n_l, S5_GROUPS, S5_GROUP)
    return y, hr, hi


def s5_bidir(u, h0r, h0i, log_dt, lam_re, lam_im, b_re, b_im, c_re, c_im, d_skip, glu_w, glu_b):
    n_b, n_l = u.shape[0], u.shape[1]
    uf = u.astype(F32).reshape(n_b, n_l, S5_GROUPS, S5_GROUP)
    ar, ai, br, bi = s5_discretize(log_dt, lam_re, lam_im, b_re, b_im)
    cr, ci = c_re.astype(F32), c_im.astype(F32)
    h0r, h0i = h0r.astype(F32), h0i.astype(F32)
    y_f, hfr, hfi = s5_scan(uf, ar[0], ai[0], br[0], bi[0], cr[0], ci[0], h0r[:, 0], h0i[:, 0])
    y_b, hbr, hbi = s5_scan(jnp.flip(uf, 1), ar[1], ai[1], br[1], bi[1], cr[1], ci[1], h0r[:, 1], h0i[:, 1])
    y = y_f + jnp.flip(y_b, 1) + d_skip.astype(F32) * uf
    y = jax.nn.gelu(y.reshape(n_b, n_l, S5_WIDTH))
    y = y * jax.nn.sigmoid(y @ glu_w.astype(F32) + glu_b.astype(F32))
    return y.astype(u.dtype), jnp.stack([hfr, hbr], axis=1), jnp.stack([hfi, hbi], axis=1)


def ab_project(h, w_in, q_norm, kv_norm, w_uq):
    n_b, n_l = h.shape[0], h.shape[1]
    z = h @ w_in
    u, q_lat, ckv, kpe = jnp.split(z, [S5_WIDTH, S5_WIDTH + MLA_Q_RANK, S5_WIDTH + MLA_Q_RANK + MLA_KV_RANK], axis=-1)
    q = (rms_norm(q_lat, q_norm) @ w_uq).reshape(n_b, n_l, MLA_HEADS, MLA_NOPE + MLA_ROPE)
    return u, q[..., :MLA_NOPE], q[..., MLA_NOPE:], rms_norm(ckv, kv_norm), kpe


def up_kv(ckv, w_ukv):
    kv = (ckv @ w_ukv).reshape(ckv.shape[:2] + (MLA_HEADS, MLA_NOPE + MLA_V_DIM))
    return kv[..., :MLA_NOPE], kv[..., MLA_NOPE:]


def mla_attend(q_nope, q_pe, k_nope, k_pe, v):
    n_b, n_q = q_nope.shape[0], q_nope.shape[1]
    n_blk = n_q // Q_BLOCK
    qn = q_nope.reshape(n_b, n_blk, Q_BLOCK, MLA_HEADS, MLA_NOPE).swapaxes(0, 1)
    qp = q_pe.reshape(n_b, n_blk, Q_BLOCK, MLA_HEADS, MLA_ROPE).swapaxes(0, 1)
    scale = (MLA_NOPE + MLA_ROPE) ** -0.5

    def block(args):
        qn_b, qp_b = args
        s = (jnp.einsum('bqhd,bkhd->bhqk', qn_b, k_nope, preferred_element_type=F32)
             + jnp.einsum('bqhr,bkr->bhqk', qp_b, k_pe, preferred_element_type=F32))
        p = jax.nn.softmax(s * scale, axis=-1).astype(v.dtype)
        return jnp.einsum('bhqk,bkhv->bqhv', p, v)

    o = lax.map(block, (qn, qp))
    return o.swapaxes(0, 1).reshape(n_b, n_q, MLA_HEADS * MLA_V_DIM)


def ab_mixer_context(h, w_in, w_out, q_norm, kv_norm, w_uq, w_ukv, s5):
    n_b = h.shape[0]
    u, q_nope, q_pe, ckv, kpe = ab_project(h, w_in, q_norm, kv_norm, w_uq)
    k_nope, v = up_kv(ckv, w_ukv)
    att = mla_attend(q_nope, q_pe, k_nope, kpe, v)
    zero = jnp.zeros((n_b, 2, S5_GROUPS, S5_STATE), F32)
    s5_out, hr, hi = s5_bidir(u, zero, zero, *s5)
    y = jnp.concatenate([s5_out, att], axis=-1) @ w_out
    return y, ckv, kpe, hr, hi


def ab_mixer_latent(h, ctx_ckv, ctx_kpe, h0r, h0i, w_in, w_out, q_norm, kv_norm, w_uq, w_ukv, s5):
    u, q_nope, q_pe, ckv, kpe = ab_project(h, w_in, q_norm, kv_norm, w_uq)
    q_pe = axial_rope(q_pe)
    kpe = axial_rope(kpe)
    kn_l, v_l = up_kv(ckv, w_ukv)
    kn_c, v_c = up_kv(ctx_ckv, w_ukv)
    att = mla_attend(q_nope, q_pe,
                     jnp.concatenate([kn_c, kn_l], axis=1),
                     jnp.concatenate([ctx_kpe.astype(kpe.dtype), kpe], axis=1),
                     jnp.concatenate([v_c, v_l], axis=1))
    s5_out, _, _ = s5_bidir(u, h0r, h0i, *s5)
    return jnp.concatenate([s5_out, att], axis=-1) @ w_out


def hgrn2_scan(q, k, logf, v, s0):
    n_b, n_l = q.shape[0], q.shape[1]
    n_c = n_l // HG_CHUNK

    def chunks(a):
        return a.reshape(n_b, n_c, HG_CHUNK, HG_HEADS, a.shape[-1]).swapaxes(0, 1)

    mask = jnp.tril(jnp.ones((HG_CHUNK, HG_CHUNK), bool))

    def body(S, xs):
        qc, kc, gc, vc = xs
        b = jnp.cumsum(gc, axis=1)
        b_end = b[:, -1]
        q_t = qc * jnp.exp(b)
        k_t = kc * jnp.exp(-b)
        k_e = kc * jnp.exp(b_end[:, None] - b)
        att = jnp.where(mask, jnp.einsum('bthk,bshk->bhts', q_t, k_t), 0.0)
        o = jnp.einsum('bhts,bshv->bthv', att, vc) + jnp.einsum('bthk,bhkv->bthv', q_t, S)
        S = jnp.exp(b_end)[..., None] * S + jnp.einsum('bthk,bthv->bhkv', k_e, vc)
        return S, o

    S, o = lax.scan(body, s0, (chunks(q), chunks(k), chunks(logf), chunks(v)))
    return o.swapaxes(0, 1).reshape(n_b, n_l, HG_HEADS, HG_V), S


def hgrn2_mixer(h, s0, w_in, w_out, lb, out_norm):
    n_b, n_l = h.shape[0], h.shape[1]
    z = h @ w_in
    q, zf, zb, v, g = jnp.split(z, [HG_KW, 2 * HG_KW, 3 * HG_KW, 3 * HG_KW + HG_VW], axis=-1)

    def heads(a, d):
        return a.astype(F32).reshape(n_b, n_l, HG_HEADS, d)

    q = heads(jax.nn.silu(q), HG_K) * (HG_K ** -0.5)
    v = heads(v, HG_V)

    def gates(zz, lbd):
        zz = heads(zz, HG_K)
        lbd = lbd.reshape(HG_HEADS, HG_K)
        k = (1.0 - lbd) * jax.nn.sigmoid(-zz)
        logf = jnp.log(lbd + (1.0 - lbd) * jax.nn.sigmoid(zz))
        return k, logf

    kf, gf = gates(zf, lb[0])
    kb, gb = gates(zb, lb[1])
    s0 = s0.astype(F32)
    o_f, s_f = hgrn2_scan(q, kf, gf, v, s0[:, 0])
    o_b, s_b = hgrn2_scan(jnp.flip(q, 1), jnp.flip(kb, 1), jnp.flip(gb, 1), jnp.flip(v, 1), s0[:, 1])
    o = o_f + jnp.flip(o_b, 1)
    o = rms_norm(o, out_norm.reshape(HG_HEADS, HG_V)).reshape(n_b, n_l, HG_VW)
    o = (o * jax.nn.silu(g.astype(F32))).astype(h.dtype)
    return o @ w_out, jnp.stack([s_f, s_b], axis=1)


def setup_inputs(seed: int = 0) -> dict:
    key = jax.random.key(seed)
    it = iter(jax.random.split(key, 48))

    def nrm(shape, scale):
        return jax.random.normal(next(it), shape, F32) * scale

    def gain(shape):
        return 1.0 + nrm(shape, 0.05)

    d = D_MODEL
    lam_im_base = jnp.pi * jnp.arange(S5_STATE, dtype=F32)
    return {
        'x_prompt': nrm((BATCH, SEQ, d), 1.0),
        'x_sample': nrm((DEC_BATCH, DEC_SEQ, d), 1.0),
        'cache_ckv': nrm((DEC_BATCH, N_EVEN, PAST_LEN, MLA_KV_RANK), 1.0),
        'cache_kpe': nrm((DEC_BATCH, N_EVEN, PAST_LEN, MLA_ROPE), 1.0),
        'state_s5_re': nrm((DEC_BATCH, N_EVEN, 2, S5_GROUPS, S5_STATE), 0.2),
        'state_s5_im': nrm((DEC_BATCH, N_EVEN, 2, S5_GROUPS, S5_STATE), 0.2),
        'state_hgrn': nrm((DEC_BATCH, N_ODD, 2, HG_HEADS, HG_K, HG_V), 0.5),
        'c': nrm((DEC_BATCH, d), 1.0),
        'c_ctx': nrm((d,), 1.0),
        'mod_w': nrm((DEPTH, d, 6 * d), d ** -0.5),
        'mod_b': nrm((DEPTH, 6 * d), 0.01),
        'norm_g': gain((DEPTH, 4, d)),
        'mlp_w1': nrm((DEPTH, d, D_FF), d ** -0.5),
        'mlp_w2': nrm((DEPTH, D_FF, d), D_FF ** -0.5),
        'ab_w_in': nrm((N_EVEN, d, AB_IN), d ** -0.5),
        'ab_w_out': nrm((N_EVEN, AB_OUT, d), AB_OUT ** -0.5),
        'mla_q_norm': gain((N_EVEN, MLA_Q_RANK)),
        'mla_kv_norm': gain((N_EVEN, MLA_KV_RANK)),
        'mla_w_uq': nrm((N_EVEN, MLA_Q_RANK, MLA_HEADS * (MLA_NOPE + MLA_ROPE)), MLA_Q_RANK ** -0.5),
        'mla_w_ukv': nrm((N_EVEN, MLA_KV_RANK, MLA_HEADS * (MLA_NOPE + MLA_V_DIM)), MLA_KV_RANK ** -0.5),
        's5_log_dt': jax.random.uniform(next(it), (N_EVEN, 2, S5_GROUPS), F32, math.log(1e-3), math.log(1e-1)),
        's5_lam_re': -0.5 + nrm((N_EVEN, 2, S5_GROUPS, S5_STATE), 0.01),
        's5_lam_im': lam_im_base + nrm((N_EVEN, 2, S5_GROUPS, S5_STATE), 0.01),
        's5_b_re': nrm((N_EVEN, 2, S5_GROUPS, S5_STATE, S5_GROUP), (2 * S5_GROUP) ** -0.5),
        's5_b_im': nrm((N_EVEN, 2, S5_GROUPS, S5_STATE, S5_GROUP), (2 * S5_GROUP) ** -0.5),
        's5_c_re': nrm((N_EVEN, 2, S5_GROUPS, S5_GROUP, S5_STATE), S5_STATE ** -0.5),
        's5_c_im': nrm((N_EVEN, 2, S5_GROUPS, S5_GROUP, S5_STATE), S5_STATE ** -0.5),
        's5_d': nrm((N_EVEN, S5_GROUPS, S5_GROUP), 0.1),
        's5_glu_w': nrm((N_EVEN, S5_WIDTH, S5_WIDTH), S5_WIDTH ** -0.5),
        's5_glu_b': nrm((N_EVEN, S5_WIDTH), 0.01),
        'hg_w_in': nrm((N_ODD, d, HG_IN), d ** -0.5),
        'hg_w_out': nrm((N_ODD, HG_VW, d), HG_VW ** -0.5),
        'hg_lower_bounds': nrm((2, DEPTH, HG_KW), 0.1),
        'hg_out_norm': gain((N_ODD, HG_VW)),
    }


def reference(x_prompt, x_sample, cache_ckv, cache_kpe, state_s5_re, state_s5_im, state_hgrn, c, c_ctx,
              mod_w, mod_b, norm_g, mlp_w1, mlp_w2, ab_w_in, ab_w_out, mla_q_norm, mla_kv_norm, mla_w_uq,
              mla_w_ukv, s5_log_dt, s5_lam_re, s5_lam_im, s5_b_re, s5_b_im, s5_c_re, s5_c_im, s5_d, s5_glu_w,
              s5_glu_b, hg_w_in, hg_w_out, hg_lower_bounds, hg_out_norm):
    lbs = jax.nn.softmax(hg_lower_bounds.astype(F32), axis=1)
    lbs = jnp.cumsum(lbs, axis=1) - lbs[:, :1]
    xp, xs = x_prompt, x_sample
    l_ckv, l_kpe, l_s5r, l_s5i, l_hg = [], [], [], [], []
    for layer in range(DEPTH):
        j = layer // 2
        shp, scp, gtp, shp2, scp2, gtp2 = modulation(c_ctx[None], mod_w[layer], mod_b[layer])
        shs, scs, gts, shs2, scs2, gts2 = modulation(c, mod_w[layer], mod_b[layer])
        hp = rms_norm(xp, norm_g[layer, 0]) * (1.0 + scp) + shp
        hs = rms_norm(xs, norm_g[layer, 0]) * (1.0 + scs) + shs
        if layer % 2 == 0:
            s5 = (s5_log_dt[j], s5_lam_re[j], s5_lam_im[j], s5_b_re[j], s5_b_im[j], s5_c_re[j], s5_c_im[j],
                  s5_d[j], s5_glu_w[j], s5_glu_b[j])
            yp, ckv, kpe, hr, hi = ab_mixer_context(hp, ab_w_in[j], ab_w_out[j], mla_q_norm[j], mla_kv_norm[j],
                                                    mla_w_uq[j], mla_w_ukv[j], s5)
            ys = ab_mixer_latent(hs, cache_ckv[:, j], cache_kpe[:, j], state_s5_re[:, j], state_s5_im[:, j],
                                 ab_w_in[j], ab_w_out[j], mla_q_norm[j], mla_kv_norm[j], mla_w_uq[j],
                                 mla_w_ukv[j], s5)
            l_ckv.append(ckv)
            l_kpe.append(kpe)
            l_s5r.append(hr)
            l_s5i.append(hi)
        else:
            zero = jnp.zeros((xp.shape[0], 2, HG_HEADS, HG_K, HG_V), F32)
            yp, st = hgrn2_mixer(hp, zero, hg_w_in[j], hg_w_out[j], lbs[:, layer], hg_out_norm[j])
            ys, _ = hgrn2_mixer(hs, state_hgrn[:, j], hg_w_in[j], hg_w_out[j], lbs[:, layer], hg_out_norm[j])
            l_hg.append(st)
        xp = xp + gtp * rms_norm(yp, norm_g[layer, 1])
        xs = xs + gts * rms_norm(ys, norm_g[layer, 1])
        hp = rms_norm(xp, norm_g[layer, 2]) * (1.0 + scp2) + shp2
        hs = rms_norm(xs, norm_g[layer, 2]) * (1.0 + scs2) + shs2
        xp = xp + gtp2 * rms_norm(sq_relu_mlp(hp, mlp_w1[layer], mlp_w2[layer]), norm_g[layer, 3])
        xs = xs + gts2 * rms_norm(sq_relu_mlp(hs, mlp_w1[layer], mlp_w2[layer]), norm_g[layer, 3])
    new_ckv = jnp.stack(l_ckv, axis=1)
    new_kpe = jnp.stack(l_kpe, axis=1)
    new_s5_re = jnp.stack(l_s5r, axis=1)
    new_s5_im = jnp.stack(l_s5i, axis=1)
    new_hgrn = jnp.stack(l_hg, axis=1)
    return (xp, xs, new_ckv, new_kpe, new_s5_re, new_s5_im, new_hgrn)
```

```python
import functools
import math

import jax
import jax.numpy as jnp
from jax import lax
from jax.experimental import pallas as pl
from jax.experimental.pallas import tpu as pltpu

F32 = jnp.float32
BF16 = jnp.bfloat16

EPS = 1e-6
GRID_W = 64
ROPE_BASE = 10000.0
LANES = 128
S5_CHUNK = 16
HG_SUB = 32
HG_BLOCK = 2 * HG_SUB
VMEM_LIMIT = 56 * 1024 * 1024


def _pick(n, prefs):
    for p in prefs:
        if n % p == 0:
            return p
    return n


def _cparams(sem):
    return pltpu.CompilerParams(dimension_semantics=sem, vmem_limit_bytes=VMEM_LIMIT)


def _sigmoid(x):
    return 1.0 / (1.0 + jnp.exp(-x))


def _silu(x):
    return x * _sigmoid(x)


def _gelu_tanh(x):
    c = math.sqrt(2.0 / math.pi)
    return 0.5 * x * (1.0 + jnp.tanh(c * (x + 0.044715 * (x * x * x))))


def _rms(x):
    return x * lax.rsqrt(jnp.mean(x * x, axis=-1, keepdims=True) + EPS)


def _mm_kernel(*refs, nk, n_aux, epilogue, a_blocked, out_blocked):
    a_ref, w_ref = refs[0], refs[1]
    aux_refs = refs[2:2 + n_aux]
    o_ref = refs[2 + n_aux]

    def load_a():
        if a_blocked:
            return jnp.concatenate([a_ref[c] for c in range(a_ref.shape[0])], axis=-1).astype(BF16)
        return a_ref[0].astype(BF16)

    def finish(acc):
        aux = []
        for r in aux_refs:
            v = r[...]
            aux.append(v)
        out = epilogue(acc, *aux) if epilogue is not None else acc
        if out_blocked:
            for c in range(o_ref.shape[0]):
                o_ref[c] = out[:, c * LANES:(c + 1) * LANES].astype(o_ref.dtype)
        else:
            o_ref[0] = out.astype(o_ref.dtype)

    if nk == 1:
        finish(jnp.dot(load_a(), w_ref[0].astype(BF16), preferred_element_type=F32))
        return

    acc_ref = refs[3 + n_aux]
    k = pl.program_id(3)

    @pl.when(k == 0)
    def _():
        acc_ref[...] = jnp.zeros_like(acc_ref)

    acc_ref[...] += jnp.dot(load_a(), w_ref[0].astype(BF16), preferred_element_type=F32)

    @pl.when(k == nk - 1)
    def _():
        finish(acc_ref[...])


def mm(a, w, *, epilogue=None, aux=(), out_dtype=F32, tm=None, tn=None, tk=None,
       a_blocked=False, out_blocked=False, name="mm"):
    G, K, N = w.shape
    M = a.shape[1]
    tm = tm or _pick(M, (1024, 512, 256, 128))
    tn = tn or _pick(N, (1024, 512, 256, 128))
    tk = tk or _pick(K, (512, 256, 128))
    assert M % tm == 0 and N % tn == 0 and K % tk == 0, (M, N, K, tm, tn, tk)
    nk = K // tk
    grid = (G, M // tm, N // tn, nk)
    if a_blocked:
        assert G == 1
        a_spec = pl.BlockSpec((tk // LANES, tm, LANES), lambda g, i, j, k: (k, i, 0))
    else:
        a_spec = pl.BlockSpec((1, tm, tk), lambda g, i, j, k: (g, i, k))
    w_spec = pl.BlockSpec((1, tk, tn), lambda g, i, j, k: (g, k, j))
    aux_arrays, aux_specs = [], []
    for arr, bshape, imap in aux:
        aux_arrays.append(arr)
        aux_specs.append(pl.BlockSpec(bshape, functools.partial(lambda g, i, j, k, f: f(g, i, j), f=imap)))
    if out_blocked:
        assert G == 1
        out_shape = jax.ShapeDtypeStruct((N // LANES, M, LANES), out_dtype)
        out_spec = pl.BlockSpec((tn // LANES, tm, LANES), lambda g, i, j, k: (j, i, 0))
    else:
        out_shape = jax.ShapeDtypeStruct((G, M, N), out_dtype)
        out_spec = pl.BlockSpec((1, tm, tn), lambda g, i, j, k: (g, i, j))
    scratch = [] if nk == 1 else [pltpu.VMEM((tm, tn), F32)]
    kern = functools.partial(_mm_kernel, nk=nk, n_aux=len(aux_arrays), epilogue=epilogue,
                             a_blocked=a_blocked, out_blocked=out_blocked)
    return pl.pallas_call(
        kern, grid=grid, in_specs=[a_spec, w_spec] + aux_specs, out_specs=out_spec,
        out_shape=out_shape, scratch_shapes=scratch, name=name,
        compiler_params=_cparams(("parallel", "parallel", "parallel", "arbitrary")),
    )(a, w, *aux_arrays)


def mm2d(a, w, **kw):
    out = mm(a[None], w[None], **kw)
    return out if kw.get("out_blocked") else out[0]


def _epi_relu2(acc):
    r = jnp.maximum(acc, 0.0)
    return r * r


def _epi_rms(acc, g):
    return _rms(acc) * g


def _epi_ckv(acc, g, *, rank):
    return jnp.concatenate([_rms(acc[:, :rank]) * g, acc[:, rank:]], axis=-1)


def _epi_table(acc, t):
    reps = acc.shape[1] // t.shape[1]
    return acc * jnp.concatenate([t] * reps, axis=-1)


def _epi_kadd(acc, kpks, t):
    kr2 = kpks[:, :LANES] * t[:, :LANES] + kpks[:, LANES:] * t[:, LANES:]
    blk = jnp.concatenate([jnp.zeros_like(kr2), kr2], axis=-1)
    reps = acc.shape[1] // blk.shape[1]
    return acc + jnp.concatenate([blk] * reps, axis=-1)


def _epi_add_gelu(acc, y0):
    return _gelu_tanh(acc + y0[0])


def _epi_glu(acc, gy, b):
    y = jnp.concatenate([gy[c] for c in range(gy.shape[0])], axis=-1).astype(F32)
    return y * _sigmoid(acc + b)


def _mod_kernel(c_ref, w_ref, b_ref, o_ref, *, nk):
    k = pl.program_id(1)

    @pl.when(k == 0)
    def _():
        o_ref[...] = jnp.zeros_like(o_ref)

    a = _silu(c_ref[...]).astype(BF16)
    o_ref[...] += jnp.dot(a, w_ref[...].astype(BF16), preferred_element_type=F32)

    @pl.when(k == nk - 1)
    def _():
        o_ref[...] += b_ref[...]


def modulation(cond, w, b):
    rows, d = cond.shape
    n = w.shape[1]
    tn = _pick(n, (2048, 1024, 512, 256, 128))
    tk = _pick(d, (1024, 512, 256, 128))
    nk = d // tk
    return pl.pallas_call(
        functools.partial(_mod_kernel, nk=nk), grid=(n // tn, nk),
        in_specs=[pl.BlockSpec((rows, tk), lambda j, k: (0, k)),
                  pl.BlockSpec((tk, tn), lambda j, k: (k, j)),
                  pl.BlockSpec((1, tn), lambda j, k: (0, j))],
        out_specs=pl.BlockSpec((rows, tn), lambda j, k: (0, j)),
        out_shape=jax.ShapeDtypeStruct((rows, n), F32), name="modulation",
        compiler_params=_cparams(("parallel", "arbitrary")),
    )(cond, w, b[None])


def _norm_mod_kernel(x_ref, g_ref, sc_ref, sh_ref, o_ref):
    y = _rms(x_ref[0]) * g_ref[...]
    o_ref[0] = (y * (1.0 + sc_ref[0]) + sh_ref[0]).astype(o_ref.dtype)


def norm_mod(x, g, sc, sh):
    nseq, L, D = x.shape
    tr = _pick(L, (256, 128))
    per_seq = sc.shape[0] == nseq and nseq > 1
    smap = (lambda b, i: (b, 0, 0)) if per_seq else (lambda b, i: (0, 0, 0))
    return pl.pallas_call(
        _norm_mod_kernel, grid=(nseq, L // tr),
        in_specs=[pl.BlockSpec((1, tr, D), lambda b, i: (b, i, 0)),
                  pl.BlockSpec((1, D), lambda b, i: (0, 0)),
                  pl.BlockSpec((1, 1, D), smap), pl.BlockSpec((1, 1, D), smap)],
        out_specs=pl.BlockSpec((1, tr, D), lambda b, i: (b, i, 0)),
        out_shape=jax.ShapeDtypeStruct((nseq, L, D), BF16), name="norm_mod",
        compiler_params=_cparams(("parallel", "parallel")),
    )(x, g[None], sc, sh)


def _resid_kernel(x_ref, y_ref, g1_ref, gt_ref, *rest, with_h):
    xn = x_ref[0] + gt_ref[0] * (_rms(y_ref[0]) * g1_ref[...])
    if with_h:
        g2_ref, sc_ref, sh_ref, xo_ref, h_ref = rest
        xo_ref[0] = xn
        h_ref[0] = ((_rms(xn) * g2_ref[...]) * (1.0 + sc_ref[0]) + sh_ref[0]).astype(h_ref.dtype)
    else:
        (xo_ref,) = rest
        xo_ref[0] = xn


def resid_norm(x, y, g1, gate, nxt=None):
    nseq, L, D = x.shape
    tr = _pick(L, (256, 128))
    per_seq = gate.shape[0] == nseq and nseq > 1
    smap = (lambda b, i: (b, 0, 0)) if per_seq else (lambda b, i: (0, 0, 0))
    row = pl.BlockSpec((1, tr, D), lambda b, i: (b, i, 0))
    vec = pl.BlockSpec((1, D), lambda b, i: (0, 0))
    mod = pl.BlockSpec((1, 1, D), smap)
    args = [x, y, g1[None], gate]
    specs = [row, row, vec, mod]
    out_shape = [jax.ShapeDtypeStruct((nseq, L, D), F32)]
    out_specs = [row]
    if nxt is not None:
        g2, sc, sh = nxt
        args += [g2[None], sc, sh]
        specs += [vec, mod, mod]
        out_shape.append(jax.ShapeDtypeStruct((nseq, L, D), BF16))
        out_specs.append(row)
    res = pl.pallas_call(
        functools.partial(_resid_kernel, with_h=nxt is not None), grid=(nseq, L // tr),
        in_specs=specs, out_specs=out_specs, out_shape=out_shape, name="resid_norm",
        compiler_params=_cparams(("parallel", "parallel")),
    )(*args)
    return (res[0], res[1]) if nxt is not None else (res[0], None)


def _attn_kernel(q_ref, k_ref, v_ref, o_ref):
    s = lax.dot_general(q_ref[0], k_ref[0], (((1,), (1,)), ((), ())), preferred_element_type=F32)
    m = jnp.max(s, axis=-1, keepdims=True)
    p = jnp.exp(s - m)
    l = jnp.sum(p, axis=-1, keepdims=True)
    o = jnp.dot(p.astype(BF16), v_ref[0], preferred_element_type=F32)
    o_ref[0] = (o / l).astype(o_ref.dtype)


def attention(q, kcat, v, heads):
    nseq, L, _ = q.shape
    Lk = kcat.shape[1]
    tq = _pick(L, (256, 128))
    return pl.pallas_call(
        _attn_kernel, grid=(nseq, heads, L // tq),
        in_specs=[pl.BlockSpec((1, tq, 2 * LANES), lambda b, h, i: (b, i, h)),
                  pl.BlockSpec((1, Lk, 2 * LANES), lambda b, h, i: (b, 0, h)),
                  pl.BlockSpec((1, Lk, LANES), lambda b, h, i: (b, 0, h))],
        out_specs=pl.BlockSpec((1, tq, LANES), lambda b, h, i: (b, i, h)),
        out_shape=jax.ShapeDtypeStruct((nseq, L, heads * LANES), BF16), name="mla_attention",
        compiler_params=_cparams(("parallel", "parallel", "arbitrary")),
    )(q, kcat, v)


def _s5_scan_kernel(g_ref, ad_ref, h0_ref, sin_ref, fin_ref, *, n, half):
    ad = ad_ref[0]
    afr, afi = ad[:, 0:half], ad[:, half:2 * half]
    abr, abi = ad[:, 2 * half:3 * half], ad[:, 3 * half:4 * half]
    h0 = h0_ref[0, 0]

    def body(k, carry):
        fr, fi, br, bi = carry
        kb = n - 1 - k
        sin_ref[0, 0, pl.ds(k, 1), 0:2 * half] = jnp.concatenate([fr, fi], axis=-1)
        sin_ref[0, 0, pl.ds(kb, 1), 2 * half:4 * half] = jnp.concatenate([br, bi], axis=-1)
        gf = g_ref[0, 0, pl.ds(k, 1), 0:2 * half]
        gb = g_ref[0, 0, pl.ds(kb, 1), 2 * half:4 * half]
        nfr = afr * fr - afi * fi + gf[:, :half]
        nfi = afr * fi + afi * fr + gf[:, half:]
        nbr = abr * br - abi * bi + gb[:, :half]
        nbi = abr * bi + abi * br + gb[:, half:]
        return nfr, nfi, nbr, nbi

    init = (h0[:, 0:half], h0[:, half:2 * half], h0[:, 2 * half:3 * half], h0[:, 3 * half:4 * half])
    fr, fi, br, bi = lax.fori_loop(0, n, body, init)
    fin_ref[0, 0] = jnp.concatenate([fr, fi, br, bi], axis=-1)


def s5_scan(g, ad, h0):
    nb, nseq, n, W = g.shape
    return pl.pallas_call(
        functools.partial(_s5_scan_kernel, n=n, half=W // 4), grid=(nb, nseq),
        in_specs=[pl.BlockSpec((1, 1, n, W), lambda j, b: (j, b, 0, 0)),
                  pl.BlockSpec((1, 1, W), lambda j, b: (j, 0, 0)),
                  pl.BlockSpec((1, 1, 1, W), lambda j, b: (j, b, 0, 0))],
        out_specs=[pl.BlockSpec((1, 1, n, W), lambda j, b: (j, b, 0, 0)),
                   pl.BlockSpec((1, 1, 1, W), lambda j, b: (j, b, 0, 0))],
        out_shape=[jax.ShapeDtypeStruct((nb, nseq, n, W), F32),
                   jax.ShapeDtypeStruct((nb, nseq, 1, W), F32)], name="s5_scan",
        compiler_params=_cparams(("parallel", "parallel")),
    )(g, ad, h0)


def _hg_direction(q, z, v, lb, st, rev):
    T = HG_BLOCK
    qa = _silu(q) * (LANES ** -0.5)
    t = jnp.exp(-jnp.abs(z))
    r = 1.0 / (1.0 + t)
    pos = z >= 0.0
    sig = jnp.where(pos, r, t * r)
    nsig = jnp.where(pos, t * r, r)
    kk = (1.0 - lb) * nsig
    g = jnp.log(lb + (1.0 - lb) * sig)

    row = lax.broadcasted_iota(jnp.int32, (T, LANES), 0)
    rin = row & (HG_SUB - 1)
    b = g
    for s in (1, 2, 4, 8, 16):
        if not rev:
            b = b + jnp.where(rin >= s, pltpu.roll(b, s, axis=0), 0.0)
        else:
            b = b + jnp.where(rin < HG_SUB - s, pltpu.roll(b, T - s, axis=0), 0.0)
    first = row < HG_SUB
    if not rev:
        b0, b1 = b[HG_SUB - 1:HG_SUB], b[T - 1:T]
    else:
        b0, b1 = b[0:1], b[HG_SUB:HG_SUB + 1]
    bsub = jnp.where(first, b0, b1)
    qh = qa * jnp.exp(b)
    kd = kk * jnp.exp(-b)
    ke = kk * jnp.exp(bsub - b)

    kcat = jnp.concatenate([kd, ke], axis=0).astype(BF16)
    att = lax.dot_general(qh.astype(BF16), kcat, (((1,), (1,)), ((), ())), preferred_element_type=F32)
    r2 = lax.broadcasted_iota(jnp.int32, (T, 2 * T), 0)
    c2 = lax.broadcasted_iota(jnp.int32, (T, 2 * T), 1)
    s2 = c2 & (T - 1)
    sub_bits = HG_SUB.bit_length() - 1
    other_sub = (r2 ^ s2) >> sub_bits
    causal = (s2 >= r2) if rev else (s2 <= r2)
    att = jnp.where(other_sub == (c2 >> (sub_bits + 1)), jnp.where(causal, att, 0.0), 0.0)
    vb = v.astype(BF16)
    o = jnp.dot(att.astype(BF16), jnp.concatenate([vb, vb], axis=0), preferred_element_type=F32)

    if not rev:
        dq = jnp.where(first, 1.0, jnp.exp(b0))
        ek = jnp.where(first, jnp.exp(b1), 1.0)
    else:
        dq = jnp.where(first, jnp.exp(b1), 1.0)
        ek = jnp.where(first, 1.0, jnp.exp(b0))
    o = o + lax.dot_general((qh * dq).astype(BF16), st.astype(BF16), (((1,), (1,)), ((), ())),
                            preferred_element_type=F32)
    upd = lax.dot_general(vb, (ke * ek).astype(BF16), (((0,), (0,)), ((), ())), preferred_element_type=F32)
    st_new = st * jnp.exp(b0 + b1) + upd
    return o, st_new


def _hg_kernel(*refs, nblk, has_init):
    if has_init:
        (qf_ref, zf_ref, vf_ref, qb_ref, zb_ref, vb_ref, lb_ref, s0_ref,
         of_ref, ob_ref, so_ref, stf_ref, stb_ref) = refs
    else:
        (qf_ref, zf_ref, vf_ref, qb_ref, zb_ref, vb_ref, lb_ref,
         of_ref, ob_ref, so_ref, stf_ref, stb_ref) = refs
    i = pl.program_id(2)

    @pl.when(i == 0)
    def _():
        if has_init:
            stf_ref[...] = s0_ref[0, 0, 0].T
            stb_ref[...] = s0_ref[0, 1, 0].T
        else:
            stf_ref[...] = jnp.zeros_like(stf_ref)
            stb_ref[...] = jnp.zeros_like(stb_ref)

    lb = lb_ref[...]
    o_f, st_f = _hg_direction(qf_ref[0], zf_ref[0], vf_ref[0], lb[0:1], stf_ref[...], False)
    o_b, st_b = _hg_direction(qb_ref[0], zb_ref[0], vb_ref[0], lb[1:2], stb_ref[...], True)
    of_ref[0] = o_f
    ob_ref[0] = o_b
    stf_ref[...] = st_f
    stb_ref[...] = st_b

    @pl.when(i == nblk - 1)
    def _():
        so_ref[0, 0, 0] = st_f.T
        so_ref[0, 1, 0] = st_b.T


def hgrn2(z, lb, s0, heads):
    nseq, L, _ = z.shape
    T = HG_BLOCK
    nblk = L // T
    H = heads

    def col(off, rev):
        if rev:
            return lambda b, h, i: (b, nblk - 1 - i, off + h)
        return lambda b, h, i: (b, i, off + h)

    tile = lambda off, rev: pl.BlockSpec((1, T, LANES), col(off, rev))
    in_specs = [tile(0, False), tile(H, False), tile(3 * H, False),
                tile(0, True), tile(2 * H, True), tile(3 * H, True),
                pl.BlockSpec((2, LANES), lambda b, h, i: (0, h))]
    args = [z, z, z, z, z, z, lb]
    st_spec = pl.BlockSpec((1, 2, 1, LANES, LANES), lambda b, h, i: (b, 0, h, 0, 0))
    if s0 is not None:
        in_specs.append(st_spec)
        args.append(s0)
    o_shape = jax.ShapeDtypeStruct((nseq, L, H * LANES), F32)
    return pl.pallas_call(
        functools.partial(_hg_kernel, nblk=nblk, has_init=s0 is not None), grid=(nseq, H, nblk),
        in_specs=in_specs,
        out_specs=[pl.BlockSpec((1, T, LANES), col(0, False)), pl.BlockSpec((1, T, LANES), col(0, True)), st_spec],
        out_shape=[o_shape, o_shape, jax.ShapeDtypeStruct((nseq, 2, H, LANES, LANES), F32)],
        scratch_shapes=[pltpu.VMEM((LANES, LANES), F32), pltpu.VMEM((LANES, LANES), F32)], name="hgrn2",
        compiler_params=_cparams(("parallel", "parallel", "arbitrary")),
    )(*args)


def _hg_post_kernel(of_ref, ob_ref, g_ref, gn_ref, o_ref):
    o = of_ref[0] + ob_ref[0]
    o_ref[0] = ((_rms(o) * gn_ref[...]) * _silu(g_ref[0])).astype(o_ref.dtype)


def hg_post(o_f, o_b, z, out_norm, heads):
    nseq, L, _ = o_f.shape
    tr = _pick(L, (1024, 512, 256, 128))
    tile = pl.BlockSpec((1, tr, LANES), lambda b, i, h: (b, i, h))
    return pl.pallas_call(
        _hg_post_kernel, grid=(nseq, L // tr, heads),
        in_specs=[tile, tile, pl.BlockSpec((1, tr, LANES), lambda b, i, h: (b, i, 4 * heads + h)),
                  pl.BlockSpec((1, LANES), lambda b, i, h: (0, h))],
        out_specs=tile, out_shape=jax.ShapeDtypeStruct(o_f.shape, BF16), name="hg_post",
        compiler_params=_cparams(("parallel", "parallel", "parallel")),
    )(o_f, o_b, z, out_norm[None])


def _rope_tables(n_l):
    rope = 64
    half = rope // 2
    rows = n_l // GRID_W
    row = jnp.repeat(jnp.arange(rows), GRID_W).astype(F32)
    col = jnp.tile(jnp.arange(GRID_W), rows).astype(F32)
    inv = ROPE_BASE ** (-jnp.arange(0, half, 2, dtype=F32) / half)
    ar = row[:, None] * inv[None]
    ac = col[:, None] * inv[None]
    cos = jnp.concatenate([jnp.cos(ar), jnp.cos(ar), jnp.cos(ac), jnp.cos(ac)], axis=-1)
    sin = jnp.concatenate([jnp.sin(ar), jnp.sin(ar), jnp.sin(ac), jnp.sin(ac)], axis=-1)
    return cos, sin


def _rot_cols(w):
    return jnp.concatenate([-w[..., 16:32], w[..., 0:16], -w[..., 48:64], w[..., 32:48]], axis=-1)


def _s5_weights(log_dt, lam_re, lam_im, b_re, b_im, c_re, c_im, d_skip):
    D = S5_CHUNK
    G, P = lam_re.shape[1], lam_re.shape[2]
    C = b_re.shape[-1]
    gl = LANES // C
    nb = G // gl
    dt = jnp.exp(log_dt.astype(F32))[..., None]
    lr, li = lam_re.astype(F32), lam_im.astype(F32)
    mag = jnp.exp(lr * dt)
    ar, ai = mag * jnp.cos(li * dt), mag * jnp.sin(li * dt)
    den = lr * lr + li * li
    cr_ = ((ar - 1.0) * lr + ai * li) / den
    ci_ = (ai * lr - (ar - 1.0) * li) / den
    br, bi = b_re.astype(F32), b_im.astype(F32)
    bbr = cr_[..., None] * br - ci_[..., None] * bi
    bbi = cr_[..., None] * bi + ci_[..., None] * br
    cr, ci = c_re.astype(F32), c_im.astype(F32)
    def powers(n):
        n = n.astype(F32)[:, None, None, None]
        pmag = jnp.exp(n * (lr * dt)[None])
        return pmag * jnp.cos(n * (li * dt)[None]), pmag * jnp.sin(n * (li * dt)[None])

    def times_b(p_r, p_i):
        return (p_r[..., None] * bbr[None] - p_i[..., None] * bbi[None],
                p_r[..., None] * bbi[None] + p_i[..., None] * bbr[None])

    steps = jnp.arange(D)
    pr, pi = powers(jnp.arange(D + 1))
    abr, abi = times_b(pr[:D], pi[:D])
    abr_dn, abi_dn = times_b(*powers(D - 1 - steps))
    pr_dn, pi_dn = powers(D - steps)
    kn = jnp.einsum('xgcp,nxgpk->nxgck', cr, abr) - jnp.einsum('xgcp,nxgpk->nxgck', ci, abi)
    s_idx = jnp.arange(D)[:, None]
    t_idx = jnp.arange(D)[None, :]
    lag = jnp.abs(t_idx - s_idx)
    kf = kn[:, 0][lag]
    kb = kn[:, 1][lag]
    fm = (t_idx >= s_idx)[..., None, None, None]
    bm = (s_idx >= t_idx)[..., None, None, None]
    tt = jnp.where(fm, kf, 0.0) + jnp.where(bm, kb, 0.0)
    eye_c = jnp.eye(C, dtype=F32)
    skip = d_skip.astype(F32)[:, :, None] * eye_c[None]
    tt = tt + jnp.where((s_idx == t_idx)[..., None, None, None], skip[None, None], 0.0)
    tt = tt.reshape(D, D, nb, gl, C, C)
    eye_g = jnp.eye(gl, dtype=F32)
    w_t = jnp.einsum('stjgoc,gh->jsgctho', tt, eye_g)
    w_t = w_t.reshape(nb, D * LANES, D * LANES)
    half = gl * P

    def to_state(xr, xi):
        xr = xr.reshape(D, nb, gl, P, C)
        xi = xi.reshape(D, nb, gl, P, C)
        wr = jnp.einsum('sjgpc,gh->jsgchp', xr, eye_g).reshape(nb, D * LANES, half)
        wi = jnp.einsum('sjgpc,gh->jsgchp', xi, eye_g).reshape(nb, D * LANES, half)
        return wr, wi

    fwr, fwi = to_state(abr_dn[:, 0], abi_dn[:, 0])
    bwr, bwi = to_state(abr[:, 1], abi[:, 1])
    w_b = jnp.concatenate([fwr, fwi, bwr, bwi], axis=-1)
    def from_state(pr_, pi_, c_r, c_i):
        re = c_r[None] * pr_[:, :, None, :] - c_i[None] * pi_[:, :, None, :]
        im = c_r[None] * pi_[:, :, None, :] + c_i[None] * pr_[:, :, None, :]
        re = re.reshape(D, nb, gl, C, P)
        im = im.reshape(D, nb, gl, C, P)
        wr = jnp.einsum('tjgcp,gh->jhptgc', re, eye_g).reshape(nb, half, D * LANES)
        wi = jnp.einsum('tjgcp,gh->jhptgc', -im, eye_g).reshape(nb, half, D * LANES)
        return wr, wi

    cfr, cfi = from_state(pr[1:D + 1, 0], pi[1:D + 1, 0], cr[0], ci[0])
    cbr, cbi = from_state(pr_dn[:, 1], pi_dn[:, 1], cr[1], ci[1])
    w_c = jnp.concatenate([cfr, cfi, cbr, cbi], axis=1)
    adr = pr[D].reshape(2, nb, half)
    adi = pi[D].reshape(2, nb, half)
    ad = jnp.concatenate([adr[0], adi[0], adr[1], adi[1]], axis=-1)[:, None, :]
    return w_t.astype(BF16), w_b.astype(BF16), w_c.astype(BF16), ad


def _ab_weights(w_in, w_out, q_norm, kv_norm, w_uq, w_ukv, s5w, q_rank, kv_rank, heads):
    rope, nope, vdim = 64, 128, 128
    o1, o2, o3 = s5w, s5w + q_rank, s5w + q_rank + kv_rank
    w_u = w_in[:, :o1].astype(BF16)
    w_q = w_in[:, o1:o2].astype(BF16)
    w_kp = w_in[:, o3:]
    w_ks = _rot_cols(w_kp)
    w_small = jnp.concatenate([w_in[:, o2:o3], w_kp, w_kp, w_ks, w_ks], axis=-1).astype(BF16)
    uq = w_uq.reshape(q_rank, heads, nope + rope)
    uq_aug = jnp.concatenate([uq[..., :nope], uq[..., nope:], _rot_cols(uq[..., nope:])], axis=-1)
    uq_aug = uq_aug.reshape(q_rank, heads * 2 * LANES).astype(BF16)
    ukv = w_ukv.reshape(kv_rank, heads, nope + vdim)
    ukn_aug = jnp.concatenate([ukv[..., :nope], jnp.zeros_like(ukv[..., :nope])], axis=-1)
    ukn_aug = ukn_aug.reshape(kv_rank, heads * 2 * LANES).astype(BF16)
    uv = ukv[..., nope:].reshape(kv_rank, heads * vdim).astype(BF16)
    return dict(w_u=w_u, w_q=w_q, w_small=w_small, uq_aug=uq_aug, ukn_aug=ukn_aug, uv=uv,
                w_out=w_out.astype(BF16), q_norm=q_norm[None], kv_norm=kv_norm[None])


def _ab_mixer(h, wts, s5m, glu_w, glu_b, *, heads, kv_rank, ctx_ckv, ctx_kpe, h0, rope):
    nseq, L, D = h.shape
    M = nseq * L
    h2 = h.reshape(M, D)
    w_t, w_b, w_c, ad = s5m
    nb = w_t.shape[0]
    scale = (128 + 64) ** -0.5

    u = mm2d(h2, wts['w_u'], out_dtype=BF16, out_blocked=True, name="ab_in_u")
    tmq = _pick(L, (1024, 512, 256, 128))
    qlat = mm2d(h2, wts['w_q'], epilogue=_epi_rms, tn=wts['w_q'].shape[1], tm=_pick(M, (512, 256, 128)),
                aux=[(wts['q_norm'], (1, wts['w_q'].shape[1]), lambda g, i, j: (0, 0))],
                out_dtype=BF16, name="ab_in_q")
    nsm = wts['w_small'].shape[1]
    small = mm2d(h2, wts['w_small'], epilogue=functools.partial(_epi_ckv, rank=kv_rank), tn=nsm,
                 tm=_pick(M, (512, 256, 128)),
                 aux=[(wts['kv_norm'], (1, kv_rank), lambda g, i, j: (0, 0))], name="ab_in_kv")
    ckv = small[:, :kv_rank]
    kpks = small[:, kv_rank:]

    ones = jnp.ones((tmq, 64), F32)
    zeros = jnp.zeros((tmq, 64), F32)
    if rope:
        cos, sin = _rope_tables(L)
    else:
        cos, sin = ones, zeros
    one128 = jnp.ones((cos.shape[0], LANES), F32)
    tq = jnp.concatenate([one128, cos, sin], axis=-1) * scale
    nq_t = tq.shape[0] // tmq
    q = mm2d(qlat, wts['uq_aug'], epilogue=_epi_table, tm=tmq,
             aux=[(tq, (tmq, 2 * LANES), lambda g, i, j: (i % nq_t, 0))], out_dtype=BF16, name="mla_uq")

    if ctx_ckv is not None:
        past = ctx_ckv.shape[1]
        ckv_all = jnp.concatenate([ctx_ckv.astype(BF16), ckv.reshape(nseq, L, kv_rank).astype(BF16)], axis=1)
        ck = ctx_kpe.astype(F32)
        ctx_kp = jnp.concatenate([ck, ck, jnp.zeros_like(ck), jnp.zeros_like(ck)], axis=-1)
        kpks_all = jnp.concatenate([ctx_kp, kpks.reshape(nseq, L, 2 * LANES)], axis=1)
        tk_ctx = jnp.concatenate([jnp.ones((past, LANES), F32), jnp.zeros((past, LANES), F32)], axis=-1)
        tk_all = jnp.concatenate([tk_ctx, jnp.concatenate([cos, cos, sin, sin], axis=-1)], axis=0)
        Lk = past + L
    else:
        ckv_all = ckv.reshape(nseq, L, kv_rank).astype(BF16)
        kpks_all = kpks.reshape(nseq, L, 2 * LANES)
        Lk = L
        tk_all = None
    tmk = _pick(Lk, (512, 256, 128))
    if tk_all is None:
        tk_all = jnp.concatenate([jnp.ones((tmk, LANES), F32), jnp.zeros((tmk, LANES), F32)], axis=-1)
    nk_t = tk_all.shape[0] // tmk
    Mk = nseq * Lk
    ckv_all = ckv_all.reshape(Mk, kv_rank)
    kcat = mm2d(ckv_all, wts['ukn_aug'], epilogue=_epi_kadd, tm=tmk,
                aux=[(kpks_all.reshape(Mk, 2 * LANES), (tmk, 2 * LANES), lambda g, i, j: (i, 0)),
                     (tk_all, (tmk, 2 * LANES), lambda g, i, j: (i % nk_t, 0))],
                out_dtype=BF16, name="mla_ukn")
    v = mm2d(ckv_all, wts['uv'], tm=tmk, out_dtype=BF16, name="mla_uv")
    att = attention(q.reshape(nseq, L, -1), kcat.reshape(nseq, Lk, -1), v.reshape(nseq, Lk, -1), heads)

    Dc = S5_CHUNK
    R = M // Dc
    n = L // Dc
    u2 = u.reshape(nb, R, Dc * LANES)
    y_intra = mm(u2, w_t, name="s5_intra")
    g = mm(u2, w_b, name="s5_to_state")
    W = g.shape[-1]
    s_in, fin = s5_scan(g.reshape(nb, nseq, n, W), ad, h0)
    tms = _pick(R, (1024, 512, 256, 128))
    tns = _pick(Dc * LANES, (1024, 512, 256, 128))
    gy = mm(s_in.reshape(nb, R, W), w_c, epilogue=_epi_add_gelu, tm=tms, tn=tns,
            aux=[(y_intra, (1, tms, tns), lambda g_, i, j: (g_, i, j))], out_dtype=BF16, name="s5_from_state")
    gyb = gy.reshape(nb, M, LANES)
    s5w = nb * LANES
    tmg = _pick(M, (1024, 512, 256, 128))
    tng = _pick(s5w, (1024, 512, 256, 128))
    s5_out = mm(gyb, glu_w.astype(BF16)[None], epilogue=_epi_glu, a_blocked=True, tm=tmg, tn=tng,
                aux=[(gyb, (tng // LANES, tmg, LANES), lambda g_, i, j: (j, i, 0)),
                     (glu_b[None], (1, tng), lambda g_, i, j: (0, j))], out_dtype=BF16, name="s5_glu")[0]

    cat = jnp.concatenate([s5_out, att.reshape(M, -1)], axis=-1)
    y = mm2d(cat, wts['w_out'], name="ab_out")
    return y.reshape(nseq, L, D), ckv, kpks[:, :64], fin


def _pack_s5_state(re, im, nb):
    nseq = re.shape[0]
    def blk(x):
        return x.reshape(nseq, nb, -1).transpose(1, 0, 2)
    parts = [blk(re[:, 0]), blk(im[:, 0]), blk(re[:, 1]), blk(im[:, 1])]
    return jnp.concatenate(parts, axis=-1)[:, :, None, :].astype(F32)


def _unpack_s5_state(fin, groups, states):
    nb, nseq = fin.shape[0], fin.shape[1]
    half = fin.shape[-1] // 4
    def blk(x):
        return x.transpose(1, 0, 2).reshape(nseq, groups, states)
    f = fin[:, :, 0]
    re = jnp.stack([blk(f[..., 0:half]), blk(f[..., 2 * half:3 * half])], axis=1)
    im = jnp.stack([blk(f[..., half:2 * half]), blk(f[..., 3 * half:])], axis=1)
    return re, im


def _hg_mixer(h, w_in, w_out, lb, out_norm, s0, heads):
    nseq, L, D = h.shape
    M = nseq * L
    z = mm2d(h.reshape(M, D), w_in, name="hg_in").reshape(nseq, L, -1)
    o_f, o_b, st = hgrn2(z, lb, s0, heads)
    o = hg_post(o_f, o_b, z, out_norm, heads)
    y = mm2d(o.reshape(M, -1), w_out, name="hg_out")
    return y.reshape(nseq, L, D), st


def _mlp(h, w1, w2):
    nseq, L, D = h.shape
    M = nseq * L
    z = mm2d(h.reshape(M, D), w1, epilogue=_epi_relu2, out_dtype=BF16, name="mlp_up")
    return mm2d(z, w2, name="mlp_down").reshape(nseq, L, D)


def kernel(x_prompt, x_sample, cache_ckv, cache_kpe, state_s5_re, state_s5_im, state_hgrn, c, c_ctx,
           mod_w, mod_b, norm_g, mlp_w1, mlp_w2, ab_w_in, ab_w_out, mla_q_norm, mla_kv_norm, mla_w_uq,
           mla_w_ukv, s5_log_dt, s5_lam_re, s5_lam_im, s5_b_re, s5_b_im, s5_c_re, s5_c_im, s5_d, s5_glu_w,
           s5_glu_b, hg_w_in, hg_w_out, hg_lower_bounds, hg_out_norm):
    depth = mod_w.shape[0]
    D = x_prompt.shape[-1]
    nsmp = x_sample.shape[0]
    s5w = s5_glu_w.shape[-1]
    q_rank = mla_q_norm.shape[-1]
    kv_rank = mla_kv_norm.shape[-1]
    mla_heads = (D - s5w) // 128
    hg_heads = hg_out_norm.shape[-1] // 128
    groups, states = s5_lam_re.shape[2], s5_lam_re.shape[3]

    lbs = jax.nn.softmax(hg_lower_bounds.astype(F32), axis=1)
    lbs = jnp.cumsum(lbs, axis=1) - lbs[:, :1]

    n_cond = nsmp + 1
    pad = (-n_cond) % 8
    cond = jnp.concatenate([c, c_ctx[None], jnp.zeros((pad, D), F32)], axis=0)

    mods = [modulation(cond, mod_w[layer], mod_b[layer]) for layer in range(depth)]

    xp, xs = x_prompt, x_sample
    hp = hs = None
    l_ckv, l_kpe, l_s5r, l_s5i, l_hg = [], [], [], [], []
    for layer in range(depth):
        j = layer // 2
        m = mods[layer]
        ms = [m[:nsmp, i * D:(i + 1) * D][:, None, :] for i in range(6)]
        mp = [m[nsmp:nsmp + 1, i * D:(i + 1) * D][:, None, :] for i in range(6)]
        if layer == 0:
            hp = norm_mod(xp, norm_g[layer, 0], mp[1], mp[0])
            hs = norm_mod(xs, norm_g[layer, 0], ms[1], ms[0])
        if layer % 2 == 0:
            wts = _ab_weights(ab_w_in[j], ab_w_out[j], mla_q_norm[j], mla_kv_norm[j], mla_w_uq[j], mla_w_ukv[j],
                              s5w, q_rank, kv_rank, mla_heads)
            s5m = _s5_weights(s5_log_dt[j], s5_lam_re[j], s5_lam_im[j], s5_b_re[j], s5_b_im[j],
                              s5_c_re[j], s5_c_im[j], s5_d[j])
            nb = s5m[0].shape[0]
            zero_h0 = jnp.zeros((nb, xp.shape[0], 1, s5m[3].shape[-1]), F32)
            yp, ckv, kpe, fin = _ab_mixer(hp, wts, s5m, s5_glu_w[j], s5_glu_b[j], heads=mla_heads,
                                          kv_rank=kv_rank, ctx_ckv=None, ctx_kpe=None, h0=zero_h0, rope=False)
            h0s = _pack_s5_state(state_s5_re[:, j], state_s5_im[:, j], nb)
            ys, _, _, _ = _ab_mixer(hs, wts, s5m, s5_glu_w[j], s5_glu_b[j], heads=mla_heads, kv_rank=kv_rank,
                                    ctx_ckv=cache_ckv[:, j], ctx_kpe=cache_kpe[:, j], h0=h0s, rope=True)
            hr, hi = _unpack_s5_state(fin, groups, states)
            l_ckv.append(ckv.reshape(xp.shape[0], xp.shape[1], kv_rank))
            l_kpe.append(kpe.reshape(xp.shape[0], xp.shape[1], 64))
            l_s5r.append(hr)
            l_s5i.append(hi)
        else:
            w_in = hg_w_in[j].astype(BF16)
            w_out = hg_w_out[j].astype(BF16)
            yp, st = _hg_mixer(hp, w_in, w_out, lbs[:, layer], hg_out_norm[j], None, hg_heads)
            ys, _ = _hg_mixer(hs, w_in, w_out, lbs[:, layer], hg_out_norm[j], state_hgrn[:, j], hg_heads)
            l_hg.append(st)
        xp, hp = resid_norm(xp, yp, norm_g[layer, 1], mp[2], (norm_g[layer, 2], mp[4], mp[3]))
        xs, hs = resid_norm(xs, ys, norm_g[layer, 1], ms[2], (norm_g[layer, 2], ms[4], ms[3]))
        w1 = mlp_w1[layer].astype(BF16)
        w2 = mlp_w2[layer].astype(BF16)
        yp = _mlp(hp, w1, w2)
        ys = _mlp(hs, w1, w2)
        if layer + 1 < depth:
            m_n = mods[layer + 1]
            nxt_s = (norm_g[layer + 1, 0], m_n[:nsmp, D:2 * D][:, None, :], m_n[:nsmp, 0:D][:, None, :])
            nxt_p = (norm_g[layer + 1, 0], m_n[nsmp:nsmp + 1, D:2 * D][:, None, :],
                     m_n[nsmp:nsmp + 1, 0:D][:, None, :])
        else:
            nxt_s = nxt_p = None
        xp, hp = resid_norm(xp, yp, norm_g[layer, 3], mp[5], nxt_p)
        xs, hs = resid_norm(xs, ys, norm_g[layer, 3], ms[5], nxt_s)
    new_ckv = jnp.stack(l_ckv, axis=1)
    new_kpe = jnp.stack(l_kpe, axis=1)
    new_s5_re = jnp.stack(l_s5r, axis=1)
    new_s5_im = jnp.stack(l_s5i, axis=1)
    new_hgrn = jnp.stack(l_hg, axis=1)
    return (xp, xs, new_ckv, new_kpe, new_s5_re, new_s5_im, new_hgrn)
```

```python
import functools
import math

import jax
import jax.numpy as jnp
from jax import lax
from jax.experimental import pallas as pl
from jax.experimental.pallas import tpu as pltpu

F32 = jnp.float32
BF16 = jnp.bfloat16

EPS = 1e-6
GRID_W = 64
ROPE_BASE = 10000.0
LANES = 128
S5_CHUNK = 16
HG_SUB = 32
HG_BLOCK = 2 * HG_SUB
VMEM_LIMIT = 56 * 1024 * 1024
MM_VMEM_BUDGET = 42 * 1024 * 1024
ATT_HEADS = 2
HG_HEADS = 4
HG_BLOCKS = 4
S5_SEQS = 4


def _pick(n, prefs):
    for p in prefs:
        if n % p == 0:
            return p
    return n


def _cparams(sem):
    return pltpu.CompilerParams(dimension_semantics=sem, vmem_limit_bytes=VMEM_LIMIT)


def _sigmoid(x):
    return 0.5 + 0.5 * jnp.tanh(0.5 * x)


def _silu(x):
    return x * _sigmoid(x)


def _gelu_tanh(x):
    c = math.sqrt(2.0 / math.pi)
    return 0.5 * x * (1.0 + jnp.tanh(c * (x + 0.044715 * (x * x * x))))


def _rms(x):
    return x * lax.rsqrt(jnp.mean(x * x, axis=-1, keepdims=True) + EPS)


def _mm_kernel(*refs, nk, n_aux, epilogue, a_blocked, out_blocked):
    a_ref, w_ref = refs[0], refs[1]
    aux_refs = refs[2:2 + n_aux]
    o_ref = refs[2 + n_aux]

    def load_a():
        if a_blocked:
            return jnp.concatenate([a_ref[c] for c in range(a_ref.shape[0])], axis=-1).astype(BF16)
        return a_ref[0].astype(BF16)

    def finish(acc):
        aux = []
        for r in aux_refs:
            v = r[...]
            aux.append(v)
        out = epilogue(acc, *aux) if epilogue is not None else acc
        if out_blocked:
            for c in range(o_ref.shape[0]):
                o_ref[c] = out[:, c * LANES:(c + 1) * LANES].astype(o_ref.dtype)
        else:
            o_ref[0] = out.astype(o_ref.dtype)

    if nk == 1:
        finish(jnp.dot(load_a(), w_ref[0].astype(BF16), preferred_element_type=F32))
        return

    acc_ref = refs[3 + n_aux]
    k = pl.program_id(3)

    @pl.when(k == 0)
    def _():
        acc_ref[...] = jnp.zeros_like(acc_ref)

    acc_ref[...] += jnp.dot(load_a(), w_ref[0].astype(BF16), preferred_element_type=F32)

    @pl.when(k == nk - 1)
    def _():
        finish(acc_ref[...])


def _pick_tk(K, tm, tn, a_bytes, w_bytes, out_bytes, aux_bytes):
    for tk in (K, 4096, 2048, 1024, 512, 256, 128):
        if tk > K or K % tk:
            continue
        acc = 0 if tk == K else 4 * tm * tn
        est = 2 * (tm * tk * a_bytes + tk * tn * w_bytes) + 2 * tm * tn * out_bytes + acc + 2 * aux_bytes
        if est <= MM_VMEM_BUDGET:
            return tk
    return LANES


def mm(a, w, *, wg=0, epilogue=None, aux=(), out_dtype=F32, tm=None, tn=None, tk=None,
       a_blocked=False, out_blocked=False, name="mm"):
    _, K, N = w.shape
    G = 1 if a_blocked else a.shape[0]
    M = a.shape[1]
    tm = tm or _pick(M, (1024, 512, 256, 128))
    tn = tn or _pick(N, (1024, 512, 256, 128))
    if tk is None:
        aux_bytes = sum(math.prod(bshape) * arr.dtype.itemsize for arr, bshape, _ in aux)
        tk = _pick_tk(K, tm, tn, a.dtype.itemsize, w.dtype.itemsize, jnp.dtype(out_dtype).itemsize, aux_bytes)
    assert M % tm == 0 and N % tn == 0 and K % tk == 0, (M, N, K, tm, tn, tk)
    nk = K // tk
    grid = (G, M // tm, N // tn, nk)
    if a_blocked:
        a_spec = pl.BlockSpec((tk // LANES, tm, LANES), lambda g, i, j, k: (k, i, 0))
    else:
        a_spec = pl.BlockSpec((1, tm, tk), lambda g, i, j, k: (g, i, k))
    w_spec = pl.BlockSpec((1, tk, tn), lambda g, i, j, k: (wg + g, k, j))
    aux_arrays, aux_specs = [], []
    for arr, bshape, imap in aux:
        aux_arrays.append(arr)
        aux_specs.append(pl.BlockSpec(bshape, functools.partial(lambda g, i, j, k, f: f(g, i, j), f=imap)))
    if out_blocked:
        assert G == 1
        out_shape = jax.ShapeDtypeStruct((N // LANES, M, LANES), out_dtype)
        out_spec = pl.BlockSpec((tn // LANES, tm, LANES), lambda g, i, j, k: (j, i, 0))
    else:
        out_shape = jax.ShapeDtypeStruct((G, M, N), out_dtype)
        out_spec = pl.BlockSpec((1, tm, tn), lambda g, i, j, k: (g, i, j))
    scratch = [] if nk == 1 else [pltpu.VMEM((tm, tn), F32)]
    kern = functools.partial(_mm_kernel, nk=nk, n_aux=len(aux_arrays), epilogue=epilogue,
                             a_blocked=a_blocked, out_blocked=out_blocked)
    return pl.pallas_call(
        kern, grid=grid, in_specs=[a_spec, w_spec] + aux_specs, out_specs=out_spec,
        out_shape=out_shape, scratch_shapes=scratch, name=name,
        compiler_params=_cparams(("parallel", "parallel", "parallel", "arbitrary")),
    )(a, w, *aux_arrays)


def mm2d(a, w, **kw):
    out = mm(a[None], w if w.ndim == 3 else w[None], **kw)
    return out if kw.get("out_blocked") else out[0]


def _epi_relu2(acc):
    r = jnp.maximum(acc, 0.0)
    return r * r


def _epi_rms(acc, g):
    return _rms(acc) * g


def _epi_ckv(acc, g, *, rank):
    return jnp.concatenate([_rms(acc[:, :rank]) * g, acc[:, rank:]], axis=-1)


def _epi_table(acc, t):
    reps = acc.shape[1] // t.shape[1]
    return acc * jnp.concatenate([t] * reps, axis=-1)


def _epi_kadd(acc, kpks, t):
    kr2 = kpks[:, :LANES] * t[:, :LANES] + kpks[:, LANES:] * t[:, LANES:]
    blk = jnp.concatenate([jnp.zeros_like(kr2), kr2], axis=-1)
    reps = acc.shape[1] // blk.shape[1]
    return acc + jnp.concatenate([blk] * reps, axis=-1)


def _epi_add_gelu(acc, y0):
    return _gelu_tanh(acc + y0[0])


def _epi_glu(acc, gy, b):
    y = jnp.concatenate([gy[c] for c in range(gy.shape[0])], axis=-1).astype(F32)
    return y * _sigmoid(acc + b)


def _mod_kernel(c_ref, w_ref, b_ref, o_ref, *, nk):
    k = pl.program_id(1)

    @pl.when(k == 0)
    def _():
        o_ref[...] = jnp.zeros_like(o_ref)

    a = _silu(c_ref[...]).astype(BF16)
    o_ref[...] += jnp.dot(a, w_ref[...].astype(BF16), preferred_element_type=F32)

    @pl.when(k == nk - 1)
    def _():
        o_ref[...] += b_ref[...]


def modulation(cond, w, b, layer):
    rows, d = cond.shape
    n = w.shape[2]
    tn = _pick(n, (2048, 1024, 512, 256, 128))
    tk = _pick(d, (1024, 512, 256, 128))
    nk = d // tk
    return pl.pallas_call(
        functools.partial(_mod_kernel, nk=nk), grid=(n // tn, nk),
        in_specs=[pl.BlockSpec((rows, tk), lambda j, k: (0, k)),
                  pl.BlockSpec((None, tk, tn), lambda j, k: (layer, k, j)),
                  pl.BlockSpec((None, 1, tn), lambda j, k: (layer, 0, j))],
        out_specs=pl.BlockSpec((rows, tn), lambda j, k: (0, j)),
        out_shape=jax.ShapeDtypeStruct((rows, n), F32), name="modulation",
        compiler_params=_cparams(("parallel", "arbitrary")),
    )(cond, w, b[:, None, :])


def _norm_mod_kernel(x_ref, g_ref, sc_ref, sh_ref, o_ref):
    y = _rms(x_ref[0]) * g_ref[...]
    o_ref[0] = (y * (1.0 + sc_ref[0]) + sh_ref[0]).astype(o_ref.dtype)


def norm_mod(x, g, sc, sh):
    nseq, L, D = x.shape
    tr = _pick(L, (256, 128))
    per_seq = sc.shape[0] == nseq and nseq > 1
    smap = (lambda b, i: (b, 0, 0)) if per_seq else (lambda b, i: (0, 0, 0))
    return pl.pallas_call(
        _norm_mod_kernel, grid=(nseq, L // tr),
        in_specs=[pl.BlockSpec((1, tr, D), lambda b, i: (b, i, 0)),
                  pl.BlockSpec((1, D), lambda b, i: (0, 0)),
                  pl.BlockSpec((1, 1, D), smap), pl.BlockSpec((1, 1, D), smap)],
        out_specs=pl.BlockSpec((1, tr, D), lambda b, i: (b, i, 0)),
        out_shape=jax.ShapeDtypeStruct((nseq, L, D), BF16), name="norm_mod",
        compiler_params=_cparams(("parallel", "parallel")),
    )(x, g[None], sc, sh)


def _resid_kernel(x_ref, y_ref, g1_ref, gt_ref, *rest, with_h):
    xn = x_ref[0] + gt_ref[0] * (_rms(y_ref[0]) * g1_ref[...])
    if with_h:
        g2_ref, sc_ref, sh_ref, xo_ref, h_ref = rest
        xo_ref[0] = xn
        h_ref[0] = ((_rms(xn) * g2_ref[...]) * (1.0 + sc_ref[0]) + sh_ref[0]).astype(h_ref.dtype)
    else:
        (xo_ref,) = rest
        xo_ref[0] = xn


def resid_norm(x, y, g1, gate, nxt=None):
    nseq, L, D = x.shape
    tr = _pick(L, (256, 128))
    per_seq = gate.shape[0] == nseq and nseq > 1
    smap = (lambda b, i: (b, 0, 0)) if per_seq else (lambda b, i: (0, 0, 0))
    row = pl.BlockSpec((1, tr, D), lambda b, i: (b, i, 0))
    vec = pl.BlockSpec((1, D), lambda b, i: (0, 0))
    mod = pl.BlockSpec((1, 1, D), smap)
    args = [x, y, g1[None], gate]
    specs = [row, row, vec, mod]
    out_shape = [jax.ShapeDtypeStruct((nseq, L, D), F32)]
    out_specs = [row]
    if nxt is not None:
        g2, sc, sh = nxt
        args += [g2[None], sc, sh]
        specs += [vec, mod, mod]
        out_shape.append(jax.ShapeDtypeStruct((nseq, L, D), BF16))
        out_specs.append(row)
    res = pl.pallas_call(
        functools.partial(_resid_kernel, with_h=nxt is not None), grid=(nseq, L // tr),
        in_specs=specs, out_specs=out_specs, out_shape=out_shape, name="resid_norm",
        compiler_params=_cparams(("parallel", "parallel")),
    )(*args)
    return (res[0], res[1]) if nxt is not None else (res[0], None)


def _attn_kernel(q_ref, k_ref, v_ref, o_ref, *, hp):
    for h in range(hp):
        qk = slice(h * 2 * LANES, (h + 1) * 2 * LANES)
        vo = slice(h * LANES, (h + 1) * LANES)
        s = lax.dot_general(q_ref[0, :, qk], k_ref[0, :, qk], (((1,), (1,)), ((), ())),
                            preferred_element_type=F32)
        m = jnp.max(s, axis=-1, keepdims=True)
        p = jnp.exp2(s - m)
        l = jnp.sum(p, axis=-1, keepdims=True)
        o = jnp.dot(p.astype(BF16), v_ref[0, :, vo], preferred_element_type=F32)
        o_ref[0, :, vo] = (o / l).astype(o_ref.dtype)


def attention(q, kcat, v, heads):
    nseq, L, _ = q.shape
    Lk = kcat.shape[1]
    tq = _pick(L, (256, 128))
    hp = ATT_HEADS if heads % ATT_HEADS == 0 else 1
    return pl.pallas_call(
        functools.partial(_attn_kernel, hp=hp), grid=(nseq, heads // hp, L // tq),
        in_specs=[pl.BlockSpec((1, tq, hp * 2 * LANES), lambda b, h, i: (b, i, h)),
                  pl.BlockSpec((1, Lk, hp * 2 * LANES), lambda b, h, i: (b, 0, h)),
                  pl.BlockSpec((1, Lk, hp * LANES), lambda b, h, i: (b, 0, h))],
        out_specs=pl.BlockSpec((1, tq, hp * LANES), lambda b, h, i: (b, i, h)),
        out_shape=jax.ShapeDtypeStruct((nseq, L, heads * LANES), BF16), name="mla_attention",
        compiler_params=_cparams(("parallel", "parallel", "arbitrary")),
    )(q, kcat, v)


def _s5_scan_kernel(g_ref, ad_ref, h0_ref, sin_ref, fin_ref, *, n, sb, half):
    ad = ad_ref[0]
    afr, afi = ad[:, 0:half], ad[:, half:2 * half]
    abr, abi = ad[:, 2 * half:3 * half], ad[:, 3 * half:4 * half]

    def body(k, carry):
        out = []
        for s in range(sb):
            fr, fi, br, bi = carry[4 * s:4 * s + 4]
            row_f = pl.ds(s * n + k, 1)
            row_b = pl.ds(s * n + n - 1 - k, 1)
            sin_ref[0, row_f, 0:2 * half] = jnp.concatenate([fr, fi], axis=-1)
            sin_ref[0, row_b, 2 * half:4 * half] = jnp.concatenate([br, bi], axis=-1)
            gf = g_ref[0, row_f, 0:2 * half]
            gb = g_ref[0, row_b, 2 * half:4 * half]
            out += [afr * fr - afi * fi + gf[:, :half], afr * fi + afi * fr + gf[:, half:],
                    abr * br - abi * bi + gb[:, :half], abr * bi + abi * br + gb[:, half:]]
        return tuple(out)

    init = []
    for s in range(sb):
        h0 = h0_ref[0, s]
        init += [h0[:, 0:half], h0[:, half:2 * half], h0[:, 2 * half:3 * half], h0[:, 3 * half:4 * half]]
    fin = lax.fori_loop(0, n, body, tuple(init))
    for s in range(sb):
        fin_ref[0, s] = jnp.concatenate(fin[4 * s:4 * s + 4], axis=-1)


def s5_scan(g, ad, h0, n):
    nb, rows, W = g.shape
    nseq = rows // n
    sb = S5_SEQS if nseq % S5_SEQS == 0 else 1
    return pl.pallas_call(
        functools.partial(_s5_scan_kernel, n=n, sb=sb, half=W // 4), grid=(nb, nseq // sb),
        in_specs=[pl.BlockSpec((1, sb * n, W), lambda j, b: (j, b, 0)),
                  pl.BlockSpec((1, 1, W), lambda j, b: (j, 0, 0)),
                  pl.BlockSpec((1, sb, 1, W), lambda j, b: (j, b, 0, 0))],
        out_specs=[pl.BlockSpec((1, sb * n, W), lambda j, b: (j, b, 0)),
                   pl.BlockSpec((1, sb, 1, W), lambda j, b: (j, b, 0, 0))],
        out_shape=[jax.ShapeDtypeStruct((nb, rows, W), F32),
                   jax.ShapeDtypeStruct((nb, nseq, 1, W), F32)], name="s5_scan",
        compiler_params=_cparams(("parallel", "parallel")),
    )(g, ad, h0)


def _hg_masks(rev):
    T = HG_BLOCK
    row = lax.broadcasted_iota(jnp.int32, (T, LANES), 0)
    rin = row & (HG_SUB - 1)
    steps = (1, 2, 4, 8, 16)
    scan = [(rin < HG_SUB - s) if rev else (rin >= s) for s in steps]
    first = row < HG_SUB
    r2 = lax.broadcasted_iota(jnp.int32, (T, 2 * T), 0)
    c2 = lax.broadcasted_iota(jnp.int32, (T, 2 * T), 1)
    s2 = c2 & (T - 1)
    sub_bits = HG_SUB.bit_length() - 1
    other_sub = (r2 ^ s2) >> sub_bits
    causal = (s2 >= r2) if rev else (s2 <= r2)
    keep = jnp.where(other_sub == (c2 >> (sub_bits + 1)), jnp.where(causal, 1, 0), 0) > 0
    return steps, scan, first, keep


def _hg_direction(q, z, v, lb, st, rev, masks):
    T = HG_BLOCK
    steps, scan, first, keep = masks
    qa = _silu(q) * (LANES ** -0.5)
    th = 0.5 * jnp.tanh(0.5 * z)
    sig = 0.5 + th
    nsig = 0.5 - th
    kk = (1.0 - lb) * nsig
    g = jnp.log(lb + (1.0 - lb) * sig)

    b = g
    for s, ok in zip(steps, scan):
        b = b + jnp.where(ok, pltpu.roll(b, (T - s) if rev else s, axis=0), 0.0)
    if not rev:
        b0, b1 = b[HG_SUB - 1:HG_SUB], b[T - 1:T]
    else:
        b0, b1 = b[0:1], b[HG_SUB:HG_SUB + 1]
    bsub = jnp.where(first, b0, b1)
    qh = qa * jnp.exp(b)
    kd = kk * jnp.exp(-b)
    ke = kk * jnp.exp(bsub - b)

    kcat = jnp.concatenate([kd, ke], axis=0).astype(BF16)
    att = lax.dot_general(qh.astype(BF16), kcat, (((1,), (1,)), ((), ())), preferred_element_type=F32)
    att = jnp.where(keep, att, 0.0)
    vb = v.astype(BF16)
    o = jnp.dot(att.astype(BF16), jnp.concatenate([vb, vb], axis=0), preferred_element_type=F32)

    if not rev:
        dq = jnp.where(first, 1.0, jnp.exp(b0))
        ek = jnp.where(first, jnp.exp(b1), 1.0)
    else:
        dq = jnp.where(first, jnp.exp(b1), 1.0)
        ek = jnp.where(first, 1.0, jnp.exp(b0))
    o = o + lax.dot_general((qh * dq).astype(BF16), st.astype(BF16), (((1,), (1,)), ((), ())),
                            preferred_element_type=F32)
    upd = lax.dot_general(vb, (ke * ek).astype(BF16), (((0,), (0,)), ((), ())), preferred_element_type=F32)
    st_new = st * jnp.exp(b0 + b1) + upd
    return o, st_new


def _hg_kernel(*refs, nsteps, nb, hp, has_init):
    if has_init:
        (qf_ref, zf_ref, vf_ref, qb_ref, zb_ref, vb_ref, lb_ref, s0_ref,
         of_ref, ob_ref, so_ref, stf_ref, stb_ref) = refs
    else:
        (qf_ref, zf_ref, vf_ref, qb_ref, zb_ref, vb_ref, lb_ref,
         of_ref, ob_ref, so_ref, stf_ref, stb_ref) = refs
    i = pl.program_id(2)
    T = HG_BLOCK

    @pl.when(i == 0)
    def _():
        for h in range(hp):
            if has_init:
                stf_ref[h] = s0_ref[0, 0, h].T
                stb_ref[h] = s0_ref[0, 1, h].T
            else:
                stf_ref[h] = jnp.zeros((LANES, LANES), F32)
                stb_ref[h] = jnp.zeros((LANES, LANES), F32)

    lb = lb_ref[...]
    masks_f = _hg_masks(False)
    masks_b = _hg_masks(True)
    for h in range(hp):
        lanes = slice(h * LANES, (h + 1) * LANES)
        st_f = stf_ref[h]
        st_b = stb_ref[h]
        for blk in range(nb):
            rows = slice(blk * T, (blk + 1) * T)
            o_f, st_f = _hg_direction(qf_ref[0, rows, lanes], zf_ref[0, rows, lanes], vf_ref[0, rows, lanes],
                                      lb[0:1, lanes], st_f, False, masks_f)
            of_ref[0, rows, lanes] = o_f
        for blk in reversed(range(nb)):
            rows = slice(blk * T, (blk + 1) * T)
            o_b, st_b = _hg_direction(qb_ref[0, rows, lanes], zb_ref[0, rows, lanes], vb_ref[0, rows, lanes],
                                      lb[1:2, lanes], st_b, True, masks_b)
            ob_ref[0, rows, lanes] = o_b
        stf_ref[h] = st_f
        stb_ref[h] = st_b

        @pl.when(i == nsteps - 1)
        def _(h=h, st_f=st_f, st_b=st_b):
            so_ref[0, 0, h] = st_f.T
            so_ref[0, 1, h] = st_b.T


def hgrn2(z, lb, s0, heads):
    nseq, L, _ = z.shape
    hp = HG_HEADS if heads % HG_HEADS == 0 else 1
    nb = HG_BLOCKS if L % (HG_BLOCKS * HG_BLOCK) == 0 else 1
    T = nb * HG_BLOCK
    nsteps = L // T
    HB = heads // hp

    def col(off, rev):
        if rev:
            return lambda b, h, i: (b, nsteps - 1 - i, off + h)
        return lambda b, h, i: (b, i, off + h)

    tile = lambda off, rev: pl.BlockSpec((1, T, hp * LANES), col(off, rev))
    in_specs = [tile(0, False), tile(HB, False), tile(3 * HB, False),
                tile(0, True), tile(2 * HB, True), tile(3 * HB, True),
                pl.BlockSpec((2, hp * LANES), lambda b, h, i: (0, h))]
    args = [z, z, z, z, z, z, lb]
    st_spec = pl.BlockSpec((1, 2, hp, LANES, LANES), lambda b, h, i: (b, 0, h, 0, 0))
    if s0 is not None:
        in_specs.append(st_spec)
        args.append(s0)
    o_shape = jax.ShapeDtypeStruct((nseq, L, heads * LANES), F32)
    st_scratch = pltpu.VMEM((hp, LANES, LANES), F32)
    return pl.pallas_call(
        functools.partial(_hg_kernel, nsteps=nsteps, nb=nb, hp=hp, has_init=s0 is not None),
        grid=(nseq, HB, nsteps), in_specs=in_specs,
        out_specs=[tile(0, False), tile(0, True), st_spec],
        out_shape=[o_shape, o_shape, jax.ShapeDtypeStruct((nseq, 2, heads, LANES, LANES), F32)],
        scratch_shapes=[st_scratch, st_scratch], name="hgrn2",
        compiler_params=_cparams(("parallel", "parallel", "arbitrary")),
    )(*args)


def _hg_post_kernel(of_ref, ob_ref, g_ref, gn_ref, o_ref):
    o = of_ref[0] + ob_ref[0]
    o_ref[0] = ((_rms(o) * gn_ref[...]) * _silu(g_ref[0])).astype(o_ref.dtype)


def hg_post(o_f, o_b, z, out_norm, heads):
    nseq, L, _ = o_f.shape
    tr = _pick(L, (1024, 512, 256, 128))
    tile = pl.BlockSpec((1, tr, LANES), lambda b, i, h: (b, i, h))
    return pl.pallas_call(
        _hg_post_kernel, grid=(nseq, L // tr, heads),
        in_specs=[tile, tile, pl.BlockSpec((1, tr, LANES), lambda b, i, h: (b, i, 4 * heads + h)),
                  pl.BlockSpec((1, LANES), lambda b, i, h: (0, h))],
        out_specs=tile, out_shape=jax.ShapeDtypeStruct(o_f.shape, BF16), name="hg_post",
        compiler_params=_cparams(("parallel", "parallel", "parallel")),
    )(o_f, o_b, z, out_norm[None])


def _rope_tables(n_l):
    rope = 64
    half = rope // 2
    rows = n_l // GRID_W
    row = jnp.repeat(jnp.arange(rows), GRID_W).astype(F32)
    col = jnp.tile(jnp.arange(GRID_W), rows).astype(F32)
    inv = ROPE_BASE ** (-jnp.arange(0, half, 2, dtype=F32) / half)
    ar = row[:, None] * inv[None]
    ac = col[:, None] * inv[None]
    cos = jnp.concatenate([jnp.cos(ar), jnp.cos(ar), jnp.cos(ac), jnp.cos(ac)], axis=-1)
    sin = jnp.concatenate([jnp.sin(ar), jnp.sin(ar), jnp.sin(ac), jnp.sin(ac)], axis=-1)
    return cos, sin


def _rot_cols(w):
    return jnp.concatenate([-w[..., 16:32], w[..., 0:16], -w[..., 48:64], w[..., 32:48]], axis=-1)


def _s5_weights(log_dt, lam_re, lam_im, b_re, b_im, c_re, c_im, d_skip):
    D = S5_CHUNK
    G, P = lam_re.shape[1], lam_re.shape[2]
    C = b_re.shape[-1]
    gl = LANES // C
    nb = G // gl
    dt = jnp.exp(log_dt.astype(F32))[..., None]
    lr, li = lam_re.astype(F32), lam_im.astype(F32)
    mag = jnp.exp(lr * dt)
    ar, ai = mag * jnp.cos(li * dt), mag * jnp.sin(li * dt)
    den = lr * lr + li * li
    cr_ = ((ar - 1.0) * lr + ai * li) / den
    ci_ = (ai * lr - (ar - 1.0) * li) / den
    br, bi = b_re.astype(F32), b_im.astype(F32)
    bbr = cr_[..., None] * br - ci_[..., None] * bi
    bbi = cr_[..., None] * bi + ci_[..., None] * br
    cr, ci = c_re.astype(F32), c_im.astype(F32)
    def powers(n):
        n = n.astype(F32)[:, None, None, None]
        pmag = jnp.exp(n * (lr * dt)[None])
        return pmag * jnp.cos(n * (li * dt)[None]), pmag * jnp.sin(n * (li * dt)[None])

    def times_b(p_r, p_i):
        return (p_r[..., None] * bbr[None] - p_i[..., None] * bbi[None],
                p_r[..., None] * bbi[None] + p_i[..., None] * bbr[None])

    steps = jnp.arange(D)
    pr, pi = powers(jnp.arange(D + 1))
    abr, abi = times_b(pr[:D], pi[:D])
    abr_dn, abi_dn = times_b(*powers(D - 1 - steps))
    pr_dn, pi_dn = powers(D - steps)
    half = gl * P
    lane_group = jnp.arange(LANES) // C
    tok_lane_group = jnp.tile(lane_group, D)
    state_group = jnp.arange(half) // P

    def response(x, a_r, a_i):
        return jnp.einsum('gcp,ngpk->ngkc', cr[x], a_r) - jnp.einsum('gcp,ngpk->ngkc', ci[x], a_i)

    kf = response(0, abr[:, 0], abi[:, 0])
    kb = response(1, abr_dn[:, 1], abi_dn[:, 1])
    skip = d_skip.astype(F32)[:, :, None] * jnp.eye(C, dtype=F32)[None]
    zpad = jnp.zeros((D - 1,) + kf.shape[1:], F32)
    lagk = (jnp.concatenate([zpad, kf], axis=0) + jnp.concatenate([kb, zpad], axis=0)
            + jnp.concatenate([zpad, skip[None], zpad], axis=0))
    bd = jnp.tile(lagk.reshape(2 * D - 1, nb, LANES, C), (1, 1, 1, gl))
    bd = jnp.where(lane_group[:, None] == lane_group[None, :], bd, 0.0).astype(BF16)
    lag_idx = jnp.arange(D)[None, :] - jnp.arange(D)[:, None] + (D - 1)
    w_t = bd[lag_idx].transpose(2, 0, 3, 1, 4).reshape(nb, D * LANES, D * LANES)

    def to_state(x):
        rows = x.reshape(D, nb, gl, P, C).transpose(1, 0, 2, 4, 3).reshape(nb, D * LANES, P)
        tiled = jnp.tile(rows, (1, 1, gl))
        return jnp.where(tok_lane_group[:, None] == state_group[None, :], tiled, 0.0).astype(BF16)

    w_b = jnp.concatenate([to_state(abr_dn[:, 0]), to_state(abi_dn[:, 0]),
                           to_state(abr[:, 1]), to_state(abi[:, 1])], axis=-1)

    def from_state(y):
        cols = y.reshape(D, nb, gl, C, P).transpose(1, 4, 0, 2, 3).reshape(nb, P, D * LANES)
        tiled = jnp.tile(cols, (1, gl, 1))
        return jnp.where(state_group[:, None] == tok_lane_group[None, :], tiled, 0.0).astype(BF16)

    def c_times(x, p_r, p_i):
        re = cr[x][None] * p_r[:, :, None, :] - ci[x][None] * p_i[:, :, None, :]
        im = cr[x][None] * p_i[:, :, None, :] + ci[x][None] * p_r[:, :, None, :]
        return re, -im

    cf = c_times(0, pr[1:D + 1, 0], pi[1:D + 1, 0])
    cb = c_times(1, pr_dn[:, 1], pi_dn[:, 1])
    w_c = jnp.concatenate([from_state(cf[0]), from_state(cf[1]), from_state(cb[0]), from_state(cb[1])], axis=1)
    adr = pr[D].reshape(2, nb, half)
    adi = pi[D].reshape(2, nb, half)
    ad = jnp.concatenate([adr[0], adi[0], adr[1], adi[1]], axis=-1)[:, None, :]
    return w_t, w_b, w_c, ad


def _ab_weights(w_in, w_out, q_norm, kv_norm, w_uq, w_ukv, s5w, q_rank, kv_rank, heads):
    rope, nope, vdim = 64, 128, 128
    o1, o2, o3 = s5w, s5w + q_rank, s5w + q_rank + kv_rank
    w_u = w_in[:, :o1].astype(BF16)
    w_q = w_in[:, o1:o2].astype(BF16)
    w_kp = w_in[:, o3:]
    w_ks = _rot_cols(w_kp)
    w_small = jnp.concatenate([w_in[:, o2:o3], w_kp, w_kp, w_ks, w_ks], axis=-1).astype(BF16)
    uq = w_uq.reshape(q_rank, heads, nope + rope)
    uq_aug = jnp.concatenate([uq[..., :nope], uq[..., nope:], _rot_cols(uq[..., nope:])], axis=-1)
    uq_aug = uq_aug.reshape(q_rank, heads * 2 * LANES).astype(BF16)
    ukv = w_ukv.reshape(kv_rank, heads, nope + vdim)
    ukn_aug = jnp.concatenate([ukv[..., :nope], jnp.zeros_like(ukv[..., :nope])], axis=-1)
    ukn_aug = ukn_aug.reshape(kv_rank, heads * 2 * LANES).astype(BF16)
    uv = ukv[..., nope:].reshape(kv_rank, heads * vdim).astype(BF16)
    return dict(w_u=w_u, w_q=w_q, w_small=w_small, uq_aug=uq_aug, ukn_aug=ukn_aug, uv=uv,
                w_out=w_out.astype(BF16), q_norm=q_norm[None], kv_norm=kv_norm[None])


def _ab_mixer(h, wts, s5m, glu_w, glu_b, *, heads, kv_rank, ctx_ckv, ctx_kpe, h0, rope):
    nseq, L, D = h.shape
    M = nseq * L
    h2 = h.reshape(M, D)
    w_t, w_b, w_c, ad = s5m
    nb = w_t.shape[0]
    scale = (128 + 64) ** -0.5 * math.log2(math.e)

    u = mm2d(h2, wts['w_u'], out_dtype=BF16, out_blocked=True, name="ab_in_u")
    tmq = _pick(L, (1024, 512, 256, 128))
    qlat = mm2d(h2, wts['w_q'], epilogue=_epi_rms, tn=wts['w_q'].shape[1], tm=_pick(M, (512, 256, 128)),
                aux=[(wts['q_norm'], (1, wts['w_q'].shape[1]), lambda g, i, j: (0, 0))],
                out_dtype=BF16, name="ab_in_q")
    nsm = wts['w_small'].shape[1]
    small = mm2d(h2, wts['w_small'], epilogue=functools.partial(_epi_ckv, rank=kv_rank), tn=nsm,
                 tm=_pick(M, (512, 256, 128)),
                 aux=[(wts['kv_norm'], (1, kv_rank), lambda g, i, j: (0, 0))], name="ab_in_kv")
    ckv = small[:, :kv_rank]
    kpks = small[:, kv_rank:]

    ones = jnp.ones((tmq, 64), F32)
    zeros = jnp.zeros((tmq, 64), F32)
    if rope:
        cos, sin = _rope_tables(L)
    else:
        cos, sin = ones, zeros
    one128 = jnp.ones((cos.shape[0], LANES), F32)
    tq = jnp.concatenate([one128, cos, sin], axis=-1) * scale
    nq_t = tq.shape[0] // tmq
    q = mm2d(qlat, wts['uq_aug'], epilogue=_epi_table, tm=tmq,
             aux=[(tq, (tmq, 2 * LANES), lambda g, i, j: (i % nq_t, 0))], out_dtype=BF16, name="mla_uq")

    if ctx_ckv is not None:
        past = ctx_ckv.shape[1]
        ckv_all = jnp.concatenate([ctx_ckv.astype(BF16), ckv.reshape(nseq, L, kv_rank).astype(BF16)], axis=1)
        ck = ctx_kpe.astype(F32)
        ctx_kp = jnp.concatenate([ck, ck, jnp.zeros_like(ck), jnp.zeros_like(ck)], axis=-1)
        kpks_all = jnp.concatenate([ctx_kp, kpks.reshape(nseq, L, 2 * LANES)], axis=1)
        tk_ctx = jnp.concatenate([jnp.ones((past, LANES), F32), jnp.zeros((past, LANES), F32)], axis=-1)
        tk_all = jnp.concatenate([tk_ctx, jnp.concatenate([cos, cos, sin, sin], axis=-1)], axis=0)
        Lk = past + L
    else:
        ckv_all = ckv.reshape(nseq, L, kv_rank).astype(BF16)
        kpks_all = kpks.reshape(nseq, L, 2 * LANES)
        Lk = L
        tk_all = None
    tmk = _pick(Lk, (512, 256, 128))
    if tk_all is None:
        tk_all = jnp.concatenate([jnp.ones((tmk, LANES), F32), jnp.zeros((tmk, LANES), F32)], axis=-1)
    nk_t = tk_all.shape[0] // tmk
    Mk = nseq * Lk
    ckv_all = ckv_all.reshape(Mk, kv_rank)
    kcat = mm2d(ckv_all, wts['ukn_aug'], epilogue=_epi_kadd, tm=tmk,
                aux=[(kpks_all.reshape(Mk, 2 * LANES), (tmk, 2 * LANES), lambda g, i, j: (i, 0)),
                     (tk_all, (tmk, 2 * LANES), lambda g, i, j: (i % nk_t, 0))],
                out_dtype=BF16, name="mla_ukn")
    v = mm2d(ckv_all, wts['uv'], tm=tmk, out_dtype=BF16, name="mla_uv")
    att = attention(q.reshape(nseq, L, -1), kcat.reshape(nseq, Lk, -1), v.reshape(nseq, Lk, -1), heads)

    Dc = S5_CHUNK
    R = M // Dc
    n = L // Dc
    u2 = u.reshape(nb, R, Dc * LANES)
    y_intra = mm(u2, w_t, name="s5_intra")
    g = mm(u2, w_b, name="s5_to_state")
    W = g.shape[-1]
    s_in, fin = s5_scan(g, ad, h0, n)
    tms = _pick(R, (1024, 512, 256, 128))
    tns = _pick(Dc * LANES, (1024, 512, 256, 128))
    gy = mm(s_in, w_c, epilogue=_epi_add_gelu, tm=tms, tn=tns,
            aux=[(y_intra, (1, tms, tns), lambda g_, i, j: (g_, i, j))], out_dtype=BF16, name="s5_from_state")
    gyb = gy.reshape(nb, M, LANES)
    s5w = nb * LANES
    tmg = _pick(M, (1024, 512, 256, 128))
    tng = _pick(s5w, (1024, 512, 256, 128))
    s5_out = mm(gyb, glu_w.astype(BF16)[None], epilogue=_epi_glu, a_blocked=True, tm=tmg, tn=tng,
                aux=[(gyb, (tng // LANES, tmg, LANES), lambda g_, i, j: (j, i, 0)),
                     (glu_b[None], (1, tng), lambda g_, i, j: (0, j))], out_dtype=BF16, name="s5_glu")[0]

    cat = jnp.concatenate([s5_out, att.reshape(M, -1)], axis=-1)
    y = mm2d(cat, wts['w_out'], name="ab_out")
    return y.reshape(nseq, L, D), ckv, kpks[:, :64], fin


def _pack_s5_state(re, im, nb):
    nseq = re.shape[0]
    def blk(x):
        return x.reshape(nseq, nb, -1).transpose(1, 0, 2)
    parts = [blk(re[:, 0]), blk(im[:, 0]), blk(re[:, 1]), blk(im[:, 1])]
    return jnp.concatenate(parts, axis=-1)[:, :, None, :].astype(F32)


def _unpack_s5_state(fin, groups, states):
    nb, nseq = fin.shape[0], fin.shape[1]
    half = fin.shape[-1] // 4
    def blk(x):
        return x.transpose(1, 0, 2).reshape(nseq, groups, states)
    f = fin[:, :, 0]
    re = jnp.stack([blk(f[..., 0:half]), blk(f[..., 2 * half:3 * half])], axis=1)
    im = jnp.stack([blk(f[..., half:2 * half]), blk(f[..., 3 * half:])], axis=1)
    return re, im


def _hg_mixer(h, w_in, w_out, j, lb, out_norm, s0, heads):
    nseq, L, D = h.shape
    M = nseq * L
    z = mm2d(h.reshape(M, D), w_in, wg=j, name="hg_in").reshape(nseq, L, -1)
    o_f, o_b, st = hgrn2(z, lb, s0, heads)
    o = hg_post(o_f, o_b, z, out_norm, heads)
    y = mm2d(o.reshape(M, -1), w_out, wg=j, name="hg_out")
    return y.reshape(nseq, L, D), st


def _mlp(h, w1, w2, layer):
    nseq, L, D = h.shape
    M = nseq * L
    z = mm2d(h.reshape(M, D), w1, wg=layer, epilogue=_epi_relu2, out_dtype=BF16, name="mlp_up")
    return mm2d(z, w2, wg=layer, name="mlp_down").reshape(nseq, L, D)


def kernel(x_prompt, x_sample, cache_ckv, cache_kpe, state_s5_re, state_s5_im, state_hgrn, c, c_ctx,
           mod_w, mod_b, norm_g, mlp_w1, mlp_w2, ab_w_in, ab_w_out, mla_q_norm, mla_kv_norm, mla_w_uq,
           mla_w_ukv, s5_log_dt, s5_lam_re, s5_lam_im, s5_b_re, s5_b_im, s5_c_re, s5_c_im, s5_d, s5_glu_w,
           s5_glu_b, hg_w_in, hg_w_out, hg_lower_bounds, hg_out_norm):
    depth = mod_w.shape[0]
    D = x_prompt.shape[-1]
    nsmp = x_sample.shape[0]
    s5w = s5_glu_w.shape[-1]
    q_rank = mla_q_norm.shape[-1]
    kv_rank = mla_kv_norm.shape[-1]
    mla_heads = (D - s5w) // 128
    hg_heads = hg_out_norm.shape[-1] // 128
    groups, states = s5_lam_re.shape[2], s5_lam_re.shape[3]

    lbs = jax.nn.softmax(hg_lower_bounds.astype(F32), axis=1)
    lbs = jnp.cumsum(lbs, axis=1) - lbs[:, :1]

    n_cond = nsmp + 1
    pad = (-n_cond) % 8
    cond = jnp.concatenate([c, c_ctx[None], jnp.zeros((pad, D), F32)], axis=0)

    mods = [modulation(cond, mod_w, mod_b, layer) for layer in range(depth)]
    w1_all, w2_all = mlp_w1.astype(BF16), mlp_w2.astype(BF16)
    hg_in_all, hg_out_all = hg_w_in.astype(BF16), hg_w_out.astype(BF16)

    xp, xs = x_prompt, x_sample
    hp = hs = None
    l_ckv, l_kpe, l_s5r, l_s5i, l_hg = [], [], [], [], []
    for layer in range(depth):
        j = layer // 2
        m = mods[layer]
        ms = [m[:nsmp, i * D:(i + 1) * D][:, None, :] for i in range(6)]
        mp = [m[nsmp:nsmp + 1, i * D:(i + 1) * D][:, None, :] for i in range(6)]
        if layer == 0:
            hp = norm_mod(xp, norm_g[layer, 0], mp[1], mp[0])
            hs = norm_mod(xs, norm_g[layer, 0], ms[1], ms[0])
        if layer % 2 == 0:
            wts = _ab_weights(ab_w_in[j], ab_w_out[j], mla_q_norm[j], mla_kv_norm[j], mla_w_uq[j], mla_w_ukv[j],
                              s5w, q_rank, kv_rank, mla_heads)
            s5m = _s5_weights(s5_log_dt[j], s5_lam_re[j], s5_lam_im[j], s5_b_re[j], s5_b_im[j],
                              s5_c_re[j], s5_c_im[j], s5_d[j])
            nb = s5m[0].shape[0]
            zero_h0 = jnp.zeros((nb, xp.shape[0], 1, s5m[3].shape[-1]), F32)
            yp, ckv, kpe, fin = _ab_mixer(hp, wts, s5m, s5_glu_w[j], s5_glu_b[j], heads=mla_heads,
                                          kv_rank=kv_rank, ctx_ckv=None, ctx_kpe=None, h0=zero_h0, rope=False)
            h0s = _pack_s5_state(state_s5_re[:, j], state_s5_im[:, j], nb)
            ys, _, _, _ = _ab_mixer(hs, wts, s5m, s5_glu_w[j], s5_glu_b[j], heads=mla_heads, kv_rank=kv_rank,
                                    ctx_ckv=cache_ckv[:, j], ctx_kpe=cache_kpe[:, j], h0=h0s, rope=True)
            hr, hi = _unpack_s5_state(fin, groups, states)
            l_ckv.append(ckv.reshape(xp.shape[0], xp.shape[1], kv_rank))
            l_kpe.append(kpe.reshape(xp.shape[0], xp.shape[1], 64))
            l_s5r.append(hr)
            l_s5i.append(hi)
        else:
            yp, st = _hg_mixer(hp, hg_in_all, hg_out_all, j, lbs[:, layer], hg_out_norm[j], None, hg_heads)
            ys, _ = _hg_mixer(hs, hg_in_all, hg_out_all, j, lbs[:, layer], hg_out_norm[j], state_hgrn[:, j],
                              hg_heads)
            l_hg.append(st)
        xp, hp = resid_norm(xp, yp, norm_g[layer, 1], mp[2], (norm_g[layer, 2], mp[4], mp[3]))
        xs, hs = resid_norm(xs, ys, norm_g[layer, 1], ms[2], (norm_g[layer, 2], ms[4], ms[3]))
        yp = _mlp(hp, w1_all, w2_all, layer)
        ys = _mlp(hs, w1_all, w2_all, layer)
        if layer + 1 < depth:
            m_n = mods[layer + 1]
            nxt_s = (norm_g[layer + 1, 0], m_n[:nsmp, D:2 * D][:, None, :], m_n[:nsmp, 0:D][:, None, :])
            nxt_p = (norm_g[layer + 1, 0], m_n[nsmp:nsmp + 1, D:2 * D][:, None, :],
                     m_n[nsmp:nsmp + 1, 0:D][:, None, :])
        else:
            nxt_s = nxt_p = None
        xp, hp = resid_norm(xp, yp, norm_g[layer, 3], mp[5], nxt_p)
        xs, hs = resid_norm(xs, ys, norm_g[layer, 3], ms[5], nxt_s)
    new_ckv = jnp.stack(l_ckv, axis=1)
    new_kpe = jnp.stack(l_kpe, axis=1)
    new_s5_re = jnp.stack(l_s5r, axis=1)
    new_s5_im = jnp.stack(l_s5i, axis=1)
    new_hgrn = jnp.stack(l_hg, axis=1)
    return (xp, xs, new_ckv, new_kpe, new_s5_re, new_s5_im, new_hgrn)
```

```python
import functools
import math

import jax
import jax.numpy as jnp
from jax import lax
from jax.experimental import pallas as pl
from jax.experimental.pallas import tpu as pltpu

F32 = jnp.float32
BF16 = jnp.bfloat16

EPS = 1e-6
GRID_W = 64
ROPE_BASE = 10000.0
LANES = 128
S5_CHUNK = 16
HG_SUB = 32
HG_BLOCK = 2 * HG_SUB
VMEM_LIMIT = 56 * 1024 * 1024
MM_VMEM_BUDGET = 42 * 1024 * 1024
ATT_HEADS = 2
HG_POST_HEADS = 8
HG_HEADS = 4
HG_BLOCKS = 16
S5_SEQS = 4


def _pick(n, prefs):
    for p in prefs:
        if n % p == 0:
            return p
    return n


def _cparams(sem):
    return pltpu.CompilerParams(dimension_semantics=sem, vmem_limit_bytes=VMEM_LIMIT)


def _sigmoid(x):
    return 0.5 + 0.5 * jnp.tanh(0.5 * x)


def _silu(x):
    return x * _sigmoid(x)


def _gelu_tanh(x):
    c = math.sqrt(2.0 / math.pi)
    return 0.5 * x * (1.0 + jnp.tanh(c * (x + 0.044715 * (x * x * x))))


def _rms(x):
    return x * lax.rsqrt(jnp.mean(x * x, axis=-1, keepdims=True) + EPS)


def _mm_kernel(*refs, nk, n_aux, epilogue, a_blocked, out_blocked):
    a_ref, w_ref = refs[0], refs[1]
    aux_refs = refs[2:2 + n_aux]
    o_ref = refs[2 + n_aux]

    def load_a():
        if a_blocked:
            return jnp.concatenate([a_ref[c] for c in range(a_ref.shape[0])], axis=-1).astype(BF16)
        return a_ref[0].astype(BF16)

    def finish(acc):
        aux = []
        for r in aux_refs:
            v = r[...]
            aux.append(v)
        out = epilogue(acc, *aux) if epilogue is not None else acc
        if out_blocked:
            for c in range(o_ref.shape[0]):
                o_ref[c] = out[:, c * LANES:(c + 1) * LANES].astype(o_ref.dtype)
        else:
            o_ref[0] = out.astype(o_ref.dtype)

    if nk == 1:
        finish(jnp.dot(load_a(), w_ref[0].astype(BF16), preferred_element_type=F32))
        return

    acc_ref = refs[3 + n_aux]
    k = pl.program_id(3)

    @pl.when(k == 0)
    def _():
        acc_ref[...] = jnp.zeros_like(acc_ref)

    acc_ref[...] += jnp.dot(load_a(), w_ref[0].astype(BF16), preferred_element_type=F32)

    @pl.when(k == nk - 1)
    def _():
        finish(acc_ref[...])


def _pick_tk(K, tm, tn, a_bytes, w_bytes, out_bytes, aux_bytes):
    for tk in (K, 4096, 2048, 1024, 512, 256, 128):
        if tk > K or K % tk:
            continue
        acc = 0 if tk == K else 4 * tm * tn
        est = 2 * (tm * tk * a_bytes + tk * tn * w_bytes) + 2 * tm * tn * out_bytes + acc + 2 * aux_bytes
        if est <= MM_VMEM_BUDGET:
            return tk
    return LANES


def mm(a, w, *, wg=0, epilogue=None, aux=(), out_dtype=F32, tm=None, tn=None, tk=None,
       a_blocked=False, out_blocked=False, name="mm"):
    _, K, N = w.shape
    G = 1 if a_blocked else a.shape[0]
    M = a.shape[1]
    tm = tm or _pick(M, (1024, 512, 256, 128))
    tn = tn or _pick(N, (1024, 512, 256, 128))
    if tk is None:
        aux_bytes = sum(math.prod(bshape) * arr.dtype.itemsize for arr, bshape, _ in aux)
        tk = _pick_tk(K, tm, tn, a.dtype.itemsize, w.dtype.itemsize, jnp.dtype(out_dtype).itemsize, aux_bytes)
    assert M % tm == 0 and N % tn == 0 and K % tk == 0, (M, N, K, tm, tn, tk)
    nk = K // tk
    grid = (G, M // tm, N // tn, nk)
    if a_blocked:
        a_spec = pl.BlockSpec((tk // LANES, tm, LANES), lambda g, i, j, k: (k, i, 0))
    else:
        a_spec = pl.BlockSpec((1, tm, tk), lambda g, i, j, k: (g, i, k))
    w_spec = pl.BlockSpec((1, tk, tn), lambda g, i, j, k: (wg + g, k, j))
    aux_arrays, aux_specs = [], []
    for arr, bshape, imap in aux:
        aux_arrays.append(arr)
        aux_specs.append(pl.BlockSpec(bshape, functools.partial(lambda g, i, j, k, f: f(g, i, j), f=imap)))
    if out_blocked:
        assert G == 1
        out_shape = jax.ShapeDtypeStruct((N // LANES, M, LANES), out_dtype)
        out_spec = pl.BlockSpec((tn // LANES, tm, LANES), lambda g, i, j, k: (j, i, 0))
    else:
        out_shape = jax.ShapeDtypeStruct((G, M, N), out_dtype)
        out_spec = pl.BlockSpec((1, tm, tn), lambda g, i, j, k: (g, i, j))
    scratch = [] if nk == 1 else [pltpu.VMEM((tm, tn), F32)]
    kern = functools.partial(_mm_kernel, nk=nk, n_aux=len(aux_arrays), epilogue=epilogue,
                             a_blocked=a_blocked, out_blocked=out_blocked)
    return pl.pallas_call(
        kern, grid=grid, in_specs=[a_spec, w_spec] + aux_specs, out_specs=out_spec,
        out_shape=out_shape, scratch_shapes=scratch, name=name,
        compiler_params=_cparams(("parallel", "parallel", "parallel", "arbitrary")),
    )(a, w, *aux_arrays)


def mm2d(a, w, **kw):
    out = mm(a[None], w if w.ndim == 3 else w[None], **kw)
    return out if kw.get("out_blocked") else out[0]


def _epi_relu2(acc):
    r = jnp.maximum(acc, 0.0)
    return r * r


def _epi_rms(acc, g):
    return _rms(acc) * g


def _epi_ckv(acc, g, *, rank):
    return jnp.concatenate([_rms(acc[:, :rank]) * g, acc[:, rank:]], axis=-1)


def _epi_table(acc, t):
    reps = acc.shape[1] // t.shape[1]
    return acc * jnp.concatenate([t] * reps, axis=-1)


def _epi_kadd(acc, kpks, t):
    kr2 = kpks[:, :LANES] * t[:, :LANES] + kpks[:, LANES:] * t[:, LANES:]
    blk = jnp.concatenate([jnp.zeros_like(kr2), kr2], axis=-1)
    reps = acc.shape[1] // blk.shape[1]
    return acc + jnp.concatenate([blk] * reps, axis=-1)


def _epi_add_gelu(acc, y0):
    return _gelu_tanh(acc + y0[0])


def _epi_glu(acc, gy, b):
    y = jnp.concatenate([gy[c] for c in range(gy.shape[0])], axis=-1).astype(F32)
    return y * _sigmoid(acc + b)


def _mod_kernel(c_ref, w_ref, b_ref, o_ref, *, nk):
    k = pl.program_id(1)

    @pl.when(k == 0)
    def _():
        o_ref[...] = jnp.zeros_like(o_ref)

    a = _silu(c_ref[...]).astype(BF16)
    o_ref[...] += jnp.dot(a, w_ref[...].astype(BF16), preferred_element_type=F32)

    @pl.when(k == nk - 1)
    def _():
        o_ref[...] += b_ref[...]


def modulation(cond, w, b, layer):
    rows, d = cond.shape
    n = w.shape[2]
    tn = _pick(n, (2048, 1024, 512, 256, 128))
    tk = _pick(d, (1024, 512, 256, 128))
    nk = d // tk
    return pl.pallas_call(
        functools.partial(_mod_kernel, nk=nk), grid=(n // tn, nk),
        in_specs=[pl.BlockSpec((rows, tk), lambda j, k: (0, k)),
                  pl.BlockSpec((None, tk, tn), lambda j, k: (layer, k, j)),
                  pl.BlockSpec((None, 1, tn), lambda j, k: (layer, 0, j))],
        out_specs=pl.BlockSpec((rows, tn), lambda j, k: (0, j)),
        out_shape=jax.ShapeDtypeStruct((rows, n), F32), name="modulation",
        compiler_params=_cparams(("parallel", "arbitrary")),
    )(cond, w, b[:, None, :])


def _norm_mod_kernel(x_ref, g_ref, sc_ref, sh_ref, o_ref):
    y = _rms(x_ref[0]) * g_ref[...]
    o_ref[0] = (y * (1.0 + sc_ref[0]) + sh_ref[0]).astype(o_ref.dtype)


def norm_mod(x, g, sc, sh):
    nseq, L, D = x.shape
    tr = _pick(L, (256, 128))
    per_seq = sc.shape[0] == nseq and nseq > 1
    smap = (lambda b, i: (b, 0, 0)) if per_seq else (lambda b, i: (0, 0, 0))
    return pl.pallas_call(
        _norm_mod_kernel, grid=(nseq, L // tr),
        in_specs=[pl.BlockSpec((1, tr, D), lambda b, i: (b, i, 0)),
                  pl.BlockSpec((1, D), lambda b, i: (0, 0)),
                  pl.BlockSpec((1, 1, D), smap), pl.BlockSpec((1, 1, D), smap)],
        out_specs=pl.BlockSpec((1, tr, D), lambda b, i: (b, i, 0)),
        out_shape=jax.ShapeDtypeStruct((nseq, L, D), BF16), name="norm_mod",
        compiler_params=_cparams(("parallel", "parallel")),
    )(x, g[None], sc, sh)


def _resid_kernel(x_ref, y_ref, g1_ref, gt_ref, *rest, with_h):
    xn = x_ref[0] + gt_ref[0] * (_rms(y_ref[0].astype(F32)) * g1_ref[...])
    if with_h:
        g2_ref, sc_ref, sh_ref, xo_ref, h_ref = rest
        xo_ref[0] = xn
        h_ref[0] = ((_rms(xn) * g2_ref[...]) * (1.0 + sc_ref[0]) + sh_ref[0]).astype(h_ref.dtype)
    else:
        (xo_ref,) = rest
        xo_ref[0] = xn


def resid_norm(x, y, g1, gate, nxt=None):
    nseq, L, D = x.shape
    tr = _pick(L, (256, 128))
    per_seq = gate.shape[0] == nseq and nseq > 1
    smap = (lambda b, i: (b, 0, 0)) if per_seq else (lambda b, i: (0, 0, 0))
    row = pl.BlockSpec((1, tr, D), lambda b, i: (b, i, 0))
    vec = pl.BlockSpec((1, D), lambda b, i: (0, 0))
    mod = pl.BlockSpec((1, 1, D), smap)
    args = [x, y, g1[None], gate]
    specs = [row, row, vec, mod]
    out_shape = [jax.ShapeDtypeStruct((nseq, L, D), F32)]
    out_specs = [row]
    if nxt is not None:
        g2, sc, sh = nxt
        args += [g2[None], sc, sh]
        specs += [vec, mod, mod]
        out_shape.append(jax.ShapeDtypeStruct((nseq, L, D), BF16))
        out_specs.append(row)
    res = pl.pallas_call(
        functools.partial(_resid_kernel, with_h=nxt is not None), grid=(nseq, L // tr),
        in_specs=specs, out_specs=out_specs, out_shape=out_shape, name="resid_norm",
        compiler_params=_cparams(("parallel", "parallel")),
    )(*args)
    return (res[0], res[1]) if nxt is not None else (res[0], None)


def _attn_kernel(q_ref, k_ref, v_ref, o_ref, *, hp):
    for h in range(hp):
        qk = slice(h * 2 * LANES, (h + 1) * 2 * LANES)
        vo = slice(h * LANES, (h + 1) * LANES)
        s = lax.dot_general(q_ref[0, :, qk], k_ref[0, :, qk], (((1,), (1,)), ((), ())),
                            preferred_element_type=F32)
        m = jnp.max(s, axis=-1, keepdims=True)
        p = jnp.exp2(s - m)
        l = jnp.sum(p, axis=-1, keepdims=True)
        o = jnp.dot(p.astype(BF16), v_ref[0, :, vo], preferred_element_type=F32)
        o_ref[0, :, vo] = (o / l).astype(o_ref.dtype)


def attention(q, kcat, v, heads):
    nseq, L, _ = q.shape
    Lk = kcat.shape[1]
    tq = _pick(L, (256, 128))
    hp = ATT_HEADS if heads % ATT_HEADS == 0 else 1
    return pl.pallas_call(
        functools.partial(_attn_kernel, hp=hp), grid=(nseq, heads // hp, L // tq),
        in_specs=[pl.BlockSpec((1, tq, hp * 2 * LANES), lambda b, h, i: (b, i, h)),
                  pl.BlockSpec((1, Lk, hp * 2 * LANES), lambda b, h, i: (b, 0, h)),
                  pl.BlockSpec((1, Lk, hp * LANES), lambda b, h, i: (b, 0, h))],
        out_specs=pl.BlockSpec((1, tq, hp * LANES), lambda b, h, i: (b, i, h)),
        out_shape=jax.ShapeDtypeStruct((nseq, L, heads * LANES), BF16), name="mla_attention",
        compiler_params=_cparams(("parallel", "parallel", "arbitrary")),
    )(q, kcat, v)


def _s5_scan_kernel(g_ref, ad_ref, h0_ref, sin_ref, fin_ref, *, n, sb, half):
    ad = ad_ref[0]
    afr, afi = ad[:, 0:half], ad[:, half:2 * half]
    abr, abi = ad[:, 2 * half:3 * half], ad[:, 3 * half:4 * half]

    def body(k, carry):
        out = []
        for s in range(sb):
            fr, fi, br, bi = carry[4 * s:4 * s + 4]
            row_f = pl.ds(s * n + k, 1)
            row_b = pl.ds(s * n + n - 1 - k, 1)
            sin_ref[0, row_f, 0:2 * half] = jnp.concatenate([fr, fi], axis=-1)
            sin_ref[0, row_b, 2 * half:4 * half] = jnp.concatenate([br, bi], axis=-1)
            gf = g_ref[0, row_f, 0:2 * half]
            gb = g_ref[0, row_b, 2 * half:4 * half]
            out += [afr * fr - afi * fi + gf[:, :half], afr * fi + afi * fr + gf[:, half:],
                    abr * br - abi * bi + gb[:, :half], abr * bi + abi * br + gb[:, half:]]
        return tuple(out)

    init = []
    for s in range(sb):
        h0 = h0_ref[0, s]
        init += [h0[:, 0:half], h0[:, half:2 * half], h0[:, 2 * half:3 * half], h0[:, 3 * half:4 * half]]
    fin = lax.fori_loop(0, n, body, tuple(init))
    for s in range(sb):
        fin_ref[0, s] = jnp.concatenate(fin[4 * s:4 * s + 4], axis=-1)


def s5_scan(g, ad, h0, n):
    nb, rows, W = g.shape
    nseq = rows // n
    sb = S5_SEQS if nseq % S5_SEQS == 0 else 1
    return pl.pallas_call(
        functools.partial(_s5_scan_kernel, n=n, sb=sb, half=W // 4), grid=(nb, nseq // sb),
        in_specs=[pl.BlockSpec((1, sb * n, W), lambda j, b: (j, b, 0)),
                  pl.BlockSpec((1, 1, W), lambda j, b: (j, 0, 0)),
                  pl.BlockSpec((1, sb, 1, W), lambda j, b: (j, b, 0, 0))],
        out_specs=[pl.BlockSpec((1, sb * n, W), lambda j, b: (j, b, 0)),
                   pl.BlockSpec((1, sb, 1, W), lambda j, b: (j, b, 0, 0))],
        out_shape=[jax.ShapeDtypeStruct((nb, rows, W), F32),
                   jax.ShapeDtypeStruct((nb, nseq, 1, W), F32)], name="s5_scan",
        compiler_params=_cparams(("parallel", "parallel")),
    )(g, ad, h0)


def _hg_masks(rev):
    T = HG_BLOCK
    row = lax.broadcasted_iota(jnp.int32, (T, LANES), 0)
    first = row < HG_SUB
    r2 = lax.broadcasted_iota(jnp.int32, (T, 2 * T), 0)
    c2 = lax.broadcasted_iota(jnp.int32, (T, 2 * T), 1)
    s2 = c2 & (T - 1)
    sub_bits = HG_SUB.bit_length() - 1
    other_sub = (r2 ^ s2) >> sub_bits
    causal = (s2 >= r2) if rev else (s2 <= r2)
    keep = jnp.where(other_sub == (c2 >> (sub_bits + 1)), jnp.where(causal, 1, 0), 0) > 0
    rin = row & (HG_SUB - 1)
    steps = (1, 2, 4, 8, 16)
    scan = [(rin < HG_SUB - s) if rev else (rin >= s) for s in steps]
    return steps, scan, first, keep


def _hg_direction(q, z, v, lb, st, rev, masks):
    T = HG_BLOCK
    steps, scan, first, keep = masks
    qa = _silu(q) * (LANES ** -0.5)
    th = 0.5 * jnp.tanh(0.5 * z)
    sig = 0.5 + th
    nsig = 0.5 - th
    kk = (1.0 - lb) * nsig
    g = jnp.log(lb + (1.0 - lb) * sig)

    b = g
    for s, ok in zip(steps, scan):
        b = b + jnp.where(ok, pltpu.roll(b, (T - s) if rev else s, axis=0), 0.0)
    if not rev:
        b0, b1 = b[HG_SUB - 1:HG_SUB], b[T - 1:T]
    else:
        b0, b1 = b[0:1], b[HG_SUB:HG_SUB + 1]
    bsub = jnp.where(first, b0, b1)
    qh = qa * jnp.exp(b)
    kd = kk * jnp.exp(-b)
    ke = kk * jnp.exp(bsub - b)

    kcat = jnp.concatenate([kd, ke], axis=0).astype(BF16)
    att = lax.dot_general(qh.astype(BF16), kcat, (((1,), (1,)), ((), ())), preferred_element_type=F32)
    att = jnp.where(keep, att, 0.0)
    vb = v.astype(BF16)
    o = jnp.dot(att.astype(BF16), jnp.concatenate([vb, vb], axis=0), preferred_element_type=F32)

    if not rev:
        dq = jnp.where(first, 1.0, jnp.exp(b0))
        ek = jnp.where(first, jnp.exp(b1), 1.0)
    else:
        dq = jnp.where(first, jnp.exp(b1), 1.0)
        ek = jnp.where(first, 1.0, jnp.exp(b0))
    o = o + lax.dot_general((qh * dq).astype(BF16), st.astype(BF16), (((1,), (1,)), ((), ())),
                            preferred_element_type=F32)
    upd = lax.dot_general(vb, (ke * ek).astype(BF16), (((0,), (0,)), ((), ())), preferred_element_type=F32)
    st_new = st * jnp.exp(b0 + b1) + upd
    return o, st_new


def _hg_kernel(*refs, nsteps, nb, hp, has_init):
    if has_init:
        (qf_ref, zf_ref, vf_ref, qb_ref, zb_ref, vb_ref, lb_ref, s0_ref,
         of_ref, ob_ref, so_ref, stf_ref, stb_ref) = refs
    else:
        (qf_ref, zf_ref, vf_ref, qb_ref, zb_ref, vb_ref, lb_ref,
         of_ref, ob_ref, so_ref, stf_ref, stb_ref) = refs
    i = pl.program_id(2)
    T = HG_BLOCK

    @pl.when(i == 0)
    def _():
        for h in range(hp):
            if has_init:
                stf_ref[h] = s0_ref[0, 0, h].T
                stb_ref[h] = s0_ref[0, 1, h].T
            else:
                stf_ref[h] = jnp.zeros((LANES, LANES), F32)
                stb_ref[h] = jnp.zeros((LANES, LANES), F32)

    lb = lb_ref[...]
    masks_f = _hg_masks(False)
    masks_b = _hg_masks(True)
    for h in range(hp):
        lanes = slice(h * LANES, (h + 1) * LANES)
        st_f = stf_ref[h]
        st_b = stb_ref[h]
        for blk in range(nb):
            rows = slice(blk * T, (blk + 1) * T)
            o_f, st_f = _hg_direction(qf_ref[0, rows, lanes], zf_ref[0, rows, lanes], vf_ref[0, rows, lanes],
                                      lb[0:1, lanes], st_f, False, masks_f)
            of_ref[0, rows, lanes] = o_f
        for blk in reversed(range(nb)):
            rows = slice(blk * T, (blk + 1) * T)
            o_b, st_b = _hg_direction(qb_ref[0, rows, lanes], zb_ref[0, rows, lanes], vb_ref[0, rows, lanes],
                                      lb[1:2, lanes], st_b, True, masks_b)
            ob_ref[0, rows, lanes] = o_b
        stf_ref[h] = st_f
        stb_ref[h] = st_b

        @pl.when(i == nsteps - 1)
        def _(h=h, st_f=st_f, st_b=st_b):
            so_ref[0, 0, h] = st_f.T
            so_ref[0, 1, h] = st_b.T


def hgrn2(z, lb, s0, heads):
    nseq, L, _ = z.shape
    hp = HG_HEADS if heads % HG_HEADS == 0 else 1
    nb = max(n for n in range(1, HG_BLOCKS + 1) if L % (n * HG_BLOCK) == 0)
    T = nb * HG_BLOCK
    nsteps = L // T
    HB = heads // hp

    def col(off, rev):
        if rev:
            return lambda b, h, i: (b, nsteps - 1 - i, off + h)
        return lambda b, h, i: (b, i, off + h)

    tile = lambda off, rev: pl.BlockSpec((1, T, hp * LANES), col(off, rev))
    in_specs = [tile(0, False), tile(HB, False), tile(3 * HB, False),
                tile(0, True), tile(2 * HB, True), tile(3 * HB, True),
                pl.BlockSpec((2, hp * LANES), lambda b, h, i: (0, h))]
    args = [z, z, z, z, z, z, lb]
    st_spec = pl.BlockSpec((1, 2, hp, LANES, LANES), lambda b, h, i: (b, 0, h, 0, 0))
    if s0 is not None:
        in_specs.append(st_spec)
        args.append(s0)
    o_shape = jax.ShapeDtypeStruct((nseq, L, heads * LANES), F32)
    st_scratch = pltpu.VMEM((hp, LANES, LANES), F32)
    return pl.pallas_call(
        functools.partial(_hg_kernel, nsteps=nsteps, nb=nb, hp=hp, has_init=s0 is not None),
        grid=(nseq, HB, nsteps), in_specs=in_specs,
        out_specs=[tile(0, False), tile(0, True), st_spec],
        out_shape=[o_shape, o_shape, jax.ShapeDtypeStruct((nseq, 2, heads, LANES, LANES), F32)],
        scratch_shapes=[st_scratch, st_scratch], name="hgrn2",
        compiler_params=_cparams(("parallel", "parallel", "arbitrary")),
    )(*args)


def _hg_post_kernel(of_ref, ob_ref, g_ref, gn_ref, o_ref, *, hp):
    for h in range(hp):
        lanes = slice(h * LANES, (h + 1) * LANES)
        o = of_ref[0, :, lanes] + ob_ref[0, :, lanes]
        o_ref[0, :, lanes] = ((_rms(o) * gn_ref[:, lanes]) * _silu(g_ref[0, :, lanes])).astype(o_ref.dtype)


def hg_post(o_f, o_b, z, out_norm, heads):
    nseq, L, _ = o_f.shape
    tr = _pick(L, (256, 128))
    hp = HG_POST_HEADS if heads % HG_POST_HEADS == 0 else 1
    tile = pl.BlockSpec((1, tr, hp * LANES), lambda b, i, h: (b, i, h))
    return pl.pallas_call(
        functools.partial(_hg_post_kernel, hp=hp), grid=(nseq, L // tr, heads // hp),
        in_specs=[tile, tile, pl.BlockSpec((1, tr, hp * LANES), lambda b, i, h: (b, i, 4 * (heads // hp) + h)),
                  pl.BlockSpec((1, hp * LANES), lambda b, i, h: (0, h))],
        out_specs=tile, out_shape=jax.ShapeDtypeStruct(o_f.shape, BF16), name="hg_post",
        compiler_params=_cparams(("parallel", "parallel", "parallel")),
    )(o_f, o_b, z, out_norm[None])


def _rope_tables(n_l):
    rope = 64
    half = rope // 2
    rows = n_l // GRID_W
    row = jnp.repeat(jnp.arange(rows), GRID_W).astype(F32)
    col = jnp.tile(jnp.arange(GRID_W), rows).astype(F32)
    inv = ROPE_BASE ** (-jnp.arange(0, half, 2, dtype=F32) / half)
    ar = row[:, None] * inv[None]
    ac = col[:, None] * inv[None]
    cos = jnp.concatenate([jnp.cos(ar), jnp.cos(ar), jnp.cos(ac), jnp.cos(ac)], axis=-1)
    sin = jnp.concatenate([jnp.sin(ar), jnp.sin(ar), jnp.sin(ac), jnp.sin(ac)], axis=-1)
    return cos, sin


def _rot_cols(w):
    return jnp.concatenate([-w[..., 16:32], w[..., 0:16], -w[..., 48:64], w[..., 32:48]], axis=-1)


def _s5_weights(log_dt, lam_re, lam_im, b_re, b_im, c_re, c_im, d_skip):
    D = S5_CHUNK
    G, P = lam_re.shape[1], lam_re.shape[2]
    C = b_re.shape[-1]
    gl = LANES // C
    nb = G // gl
    dt = jnp.exp(log_dt.astype(F32))[..., None]
    lr, li = lam_re.astype(F32), lam_im.astype(F32)
    mag = jnp.exp(lr * dt)
    ar, ai = mag * jnp.cos(li * dt), mag * jnp.sin(li * dt)
    den = lr * lr + li * li
    cr_ = ((ar - 1.0) * lr + ai * li) / den
    ci_ = (ai * lr - (ar - 1.0) * li) / den
    br, bi = b_re.astype(F32), b_im.astype(F32)
    bbr = cr_[..., None] * br - ci_[..., None] * bi
    bbi = cr_[..., None] * bi + ci_[..., None] * br
    cr, ci = c_re.astype(F32), c_im.astype(F32)
    def powers(n):
        n = n.astype(F32)[:, None, None, None]
        pmag = jnp.exp(n * (lr * dt)[None])
        return pmag * jnp.cos(n * (li * dt)[None]), pmag * jnp.sin(n * (li * dt)[None])

    def times_b(p_r, p_i):
        return (p_r[..., None] * bbr[None] - p_i[..., None] * bbi[None],
                p_r[..., None] * bbi[None] + p_i[..., None] * bbr[None])

    steps = jnp.arange(D)
    pr, pi = powers(jnp.arange(D + 1))
    abr, abi = times_b(pr[:D], pi[:D])
    abr_dn, abi_dn = times_b(*powers(D - 1 - steps))
    pr_dn, pi_dn = powers(D - steps)
    half = gl * P
    lane_group = jnp.arange(LANES) // C
    tok_lane_group = jnp.tile(lane_group, D)
    state_group = jnp.arange(half) // P

    def response(x, a_r, a_i):
        return jnp.einsum('gcp,ngpk->ngkc', cr[x], a_r) - jnp.einsum('gcp,ngpk->ngkc', ci[x], a_i)

    kf = response(0, abr[:, 0], abi[:, 0])
    kb = response(1, abr_dn[:, 1], abi_dn[:, 1])
    skip = d_skip.astype(F32)[:, :, None] * jnp.eye(C, dtype=F32)[None]
    zpad = jnp.zeros((D - 1,) + kf.shape[1:], F32)
    lagk = (jnp.concatenate([zpad, kf], axis=0) + jnp.concatenate([kb, zpad], axis=0)
            + jnp.concatenate([zpad, skip[None], zpad], axis=0))
    bd = jnp.tile(lagk.reshape(2 * D - 1, nb, LANES, C), (1, 1, 1, gl))
    bd = jnp.where(lane_group[:, None] == lane_group[None, :], bd, 0.0).astype(BF16)
    lag_idx = jnp.arange(D)[None, :] - jnp.arange(D)[:, None] + (D - 1)
    w_t = bd[lag_idx].transpose(2, 0, 3, 1, 4).reshape(nb, D * LANES, D * LANES)

    def to_state(x):
        rows = x.reshape(D, nb, gl, P, C).transpose(1, 0, 2, 4, 3).reshape(nb, D * LANES, P)
        tiled = jnp.tile(rows, (1, 1, gl))
        return jnp.where(tok_lane_group[:, None] == state_group[None, :], tiled, 0.0).astype(BF16)

    w_b = jnp.concatenate([to_state(abr_dn[:, 0]), to_state(abi_dn[:, 0]),
                           to_state(abr[:, 1]), to_state(abi[:, 1])], axis=-1)

    def from_state(y):
        cols = y.reshape(D, nb, gl, C, P).transpose(1, 4, 0, 2, 3).reshape(nb, P, D * LANES)
        tiled = jnp.tile(cols, (1, gl, 1))
        return jnp.where(state_group[:, None] == tok_lane_group[None, :], tiled, 0.0).astype(BF16)

    def c_times(x, p_r, p_i):
        re = cr[x][None] * p_r[:, :, None, :] - ci[x][None] * p_i[:, :, None, :]
        im = cr[x][None] * p_i[:, :, None, :] + ci[x][None] * p_r[:, :, None, :]
        return re, -im

    cf = c_times(0, pr[1:D + 1, 0], pi[1:D + 1, 0])
    cb = c_times(1, pr_dn[:, 1], pi_dn[:, 1])
    w_c = jnp.concatenate([from_state(cf[0]), from_state(cf[1]), from_state(cb[0]), from_state(cb[1])], axis=1)
    adr = pr[D].reshape(2, nb, half)
    adi = pi[D].reshape(2, nb, half)
    ad = jnp.concatenate([adr[0], adi[0], adr[1], adi[1]], axis=-1)[:, None, :]
    return w_t, w_b, w_c, ad


def _ab_weights(w_in, w_out, q_norm, kv_norm, w_uq, w_ukv, s5w, q_rank, kv_rank, heads):
    rope, nope, vdim = 64, 128, 128
    o1, o2, o3 = s5w, s5w + q_rank, s5w + q_rank + kv_rank
    w_u = w_in[:, :o1].astype(BF16)
    w_q = w_in[:, o1:o2].astype(BF16)
    w_kp = w_in[:, o3:]
    w_ks = _rot_cols(w_kp)
    w_small = jnp.concatenate([w_in[:, o2:o3], w_kp, w_kp, w_ks, w_ks], axis=-1).astype(BF16)
    uq = w_uq.reshape(q_rank, heads, nope + rope)
    uq_aug = jnp.concatenate([uq[..., :nope], uq[..., nope:], _rot_cols(uq[..., nope:])], axis=-1)
    uq_aug = uq_aug.reshape(q_rank, heads * 2 * LANES).astype(BF16)
    ukv = w_ukv.reshape(kv_rank, heads, nope + vdim)
    ukn_aug = jnp.concatenate([ukv[..., :nope], jnp.zeros_like(ukv[..., :nope])], axis=-1)
    ukn_aug = ukn_aug.reshape(kv_rank, heads * 2 * LANES).astype(BF16)
    uv = ukv[..., nope:].reshape(kv_rank, heads * vdim).astype(BF16)
    return dict(w_u=w_u, w_q=w_q, w_small=w_small, uq_aug=uq_aug, ukn_aug=ukn_aug, uv=uv,
                w_out=w_out.astype(BF16), q_norm=q_norm[None], kv_norm=kv_norm[None])


def _ab_mixer(h, wts, s5m, glu_w, glu_b, *, heads, kv_rank, ctx_ckv, ctx_kpe, h0, rope):
    nseq, L, D = h.shape
    M = nseq * L
    h2 = h.reshape(M, D)
    w_t, w_b, w_c, ad = s5m
    nb = w_t.shape[0]
    scale = (128 + 64) ** -0.5 * math.log2(math.e)

    u = mm2d(h2, wts['w_u'], out_dtype=BF16, out_blocked=True, name="ab_in_u")
    tmq = _pick(L, (1024, 512, 256, 128))
    qlat = mm2d(h2, wts['w_q'], epilogue=_epi_rms, tn=wts['w_q'].shape[1], tm=_pick(M, (512, 256, 128)),
                aux=[(wts['q_norm'], (1, wts['w_q'].shape[1]), lambda g, i, j: (0, 0))],
                out_dtype=BF16, name="ab_in_q")
    nsm = wts['w_small'].shape[1]
    small = mm2d(h2, wts['w_small'], epilogue=functools.partial(_epi_ckv, rank=kv_rank), tn=nsm,
                 tm=_pick(M, (512, 256, 128)),
                 aux=[(wts['kv_norm'], (1, kv_rank), lambda g, i, j: (0, 0))], name="ab_in_kv")
    ckv = small[:, :kv_rank]
    kpks = small[:, kv_rank:]

    ones = jnp.ones((tmq, 64), F32)
    zeros = jnp.zeros((tmq, 64), F32)
    if rope:
        cos, sin = _rope_tables(L)
    else:
        cos, sin = ones, zeros
    one128 = jnp.ones((cos.shape[0], LANES), F32)
    tq = jnp.concatenate([one128, cos, sin], axis=-1) * scale
    nq_t = tq.shape[0] // tmq
    q = mm2d(qlat, wts['uq_aug'], epilogue=_epi_table, tm=tmq,
             aux=[(tq, (tmq, 2 * LANES), lambda g, i, j: (i % nq_t, 0))], out_dtype=BF16, name="mla_uq")

    if ctx_ckv is not None:
        past = ctx_ckv.shape[1]
        ckv_all = jnp.concatenate([ctx_ckv.astype(BF16), ckv.reshape(nseq, L, kv_rank).astype(BF16)], axis=1)
        ck = ctx_kpe.astype(F32)
        ctx_kp = jnp.concatenate([ck, ck, jnp.zeros_like(ck), jnp.zeros_like(ck)], axis=-1)
        kpks_all = jnp.concatenate([ctx_kp, kpks.reshape(nseq, L, 2 * LANES)], axis=1)
        tk_ctx = jnp.concatenate([jnp.ones((past, LANES), F32), jnp.zeros((past, LANES), F32)], axis=-1)
        tk_all = jnp.concatenate([tk_ctx, jnp.concatenate([cos, cos, sin, sin], axis=-1)], axis=0)
        Lk = past + L
    else:
        ckv_all = ckv.reshape(nseq, L, kv_rank).astype(BF16)
        kpks_all = kpks.reshape(nseq, L, 2 * LANES)
        Lk = L
        tk_all = None
    tmk = _pick(Lk, (512, 256, 128))
    if tk_all is None:
        tk_all = jnp.concatenate([jnp.ones((tmk, LANES), F32), jnp.zeros((tmk, LANES), F32)], axis=-1)
    nk_t = tk_all.shape[0] // tmk
    Mk = nseq * Lk
    ckv_all = ckv_all.reshape(Mk, kv_rank)
    kcat = mm2d(ckv_all, wts['ukn_aug'], epilogue=_epi_kadd, tm=tmk,
                aux=[(kpks_all.reshape(Mk, 2 * LANES), (tmk, 2 * LANES), lambda g, i, j: (i, 0)),
                     (tk_all, (tmk, 2 * LANES), lambda g, i, j: (i % nk_t, 0))],
                out_dtype=BF16, name="mla_ukn")
    v = mm2d(ckv_all, wts['uv'], tm=tmk, out_dtype=BF16, name="mla_uv")
    att = attention(q.reshape(nseq, L, -1), kcat.reshape(nseq, Lk, -1), v.reshape(nseq, Lk, -1), heads)

    Dc = S5_CHUNK
    R = M // Dc
    n = L // Dc
    u2 = u.reshape(nb, R, Dc * LANES)
    y_intra = mm(u2, w_t, name="s5_intra")
    g = mm(u2, w_b, name="s5_to_state")
    W = g.shape[-1]
    s_in, fin = s5_scan(g, ad, h0, n)
    tms = _pick(R, (1024, 512, 256, 128))
    tns = _pick(Dc * LANES, (1024, 512, 256, 128))
    gy = mm(s_in, w_c, epilogue=_epi_add_gelu, tm=tms, tn=tns,
            aux=[(y_intra, (1, tms, tns), lambda g_, i, j: (g_, i, j))], out_dtype=BF16, name="s5_from_state")
    gyb = gy.reshape(nb, M, LANES)
    s5w = nb * LANES
    tmg = _pick(M, (1024, 512, 256, 128))
    tng = _pick(s5w, (1024, 512, 256, 128))
    s5_out = mm(gyb, glu_w.astype(BF16)[None], epilogue=_epi_glu, a_blocked=True, tm=tmg, tn=tng,
                aux=[(gyb, (tng // LANES, tmg, LANES), lambda g_, i, j: (j, i, 0)),
                     (glu_b[None], (1, tng), lambda g_, i, j: (0, j))], out_dtype=BF16, name="s5_glu")[0]

    cat = jnp.concatenate([s5_out, att.reshape(M, -1)], axis=-1)
    y = mm2d(cat, wts['w_out'], out_dtype=BF16, name="ab_out")
    return y.reshape(nseq, L, D), ckv, kpks[:, :64], fin


def _pack_s5_state(re, im, nb):
    nseq = re.shape[0]
    def blk(x):
        return x.reshape(nseq, nb, -1).transpose(1, 0, 2)
    parts = [blk(re[:, 0]), blk(im[:, 0]), blk(re[:, 1]), blk(im[:, 1])]
    return jnp.concatenate(parts, axis=-1)[:, :, None, :].astype(F32)


def _unpack_s5_state(fin, groups, states):
    nb, nseq = fin.shape[0], fin.shape[1]
    half = fin.shape[-1] // 4
    def blk(x):
        return x.transpose(1, 0, 2).reshape(nseq, groups, states)
    f = fin[:, :, 0]
    re = jnp.stack([blk(f[..., 0:half]), blk(f[..., 2 * half:3 * half])], axis=1)
    im = jnp.stack([blk(f[..., half:2 * half]), blk(f[..., 3 * half:])], axis=1)
    return re, im


def _hg_mixer(h, w_in, w_out, j, lb, out_norm, s0, heads):
    nseq, L, D = h.shape
    M = nseq * L
    z = mm2d(h.reshape(M, D), w_in, wg=j, name="hg_in").reshape(nseq, L, -1)
    o_f, o_b, st = hgrn2(z, lb, s0, heads)
    o = hg_post(o_f, o_b, z, out_norm, heads)
    y = mm2d(o.reshape(M, -1), w_out, wg=j, out_dtype=BF16, name="hg_out")
    return y.reshape(nseq, L, D), st


def _mlp(h, w1, w2, layer):
    nseq, L, D = h.shape
    M = nseq * L
    z = mm2d(h.reshape(M, D), w1, wg=layer, epilogue=_epi_relu2, out_dtype=BF16, name="mlp_up")
    return mm2d(z, w2, wg=layer, out_dtype=BF16, name="mlp_down").reshape(nseq, L, D)


def kernel(x_prompt, x_sample, cache_ckv, cache_kpe, state_s5_re, state_s5_im, state_hgrn, c, c_ctx,
           mod_w, mod_b, norm_g, mlp_w1, mlp_w2, ab_w_in, ab_w_out, mla_q_norm, mla_kv_norm, mla_w_uq,
           mla_w_ukv, s5_log_dt, s5_lam_re, s5_lam_im, s5_b_re, s5_b_im, s5_c_re, s5_c_im, s5_d, s5_glu_w,
           s5_glu_b, hg_w_in, hg_w_out, hg_lower_bounds, hg_out_norm):
    depth = mod_w.shape[0]
    D = x_prompt.shape[-1]
    nsmp = x_sample.shape[0]
    s5w = s5_glu_w.shape[-1]
    q_rank = mla_q_norm.shape[-1]
    kv_rank = mla_kv_norm.shape[-1]
    mla_heads = (D - s5w) // 128
    hg_heads = hg_out_norm.shape[-1] // 128
    groups, states = s5_lam_re.shape[2], s5_lam_re.shape[3]

    lbs = jax.nn.softmax(hg_lower_bounds.astype(F32), axis=1)
    lbs = jnp.cumsum(lbs, axis=1) - lbs[:, :1]

    n_cond = nsmp + 1
    pad = (-n_cond) % 8
    cond = jnp.concatenate([c, c_ctx[None], jnp.zeros((pad, D), F32)], axis=0)

    mods = [modulation(cond, mod_w, mod_b, layer) for layer in range(depth)]
    w1_all, w2_all = mlp_w1.astype(BF16), mlp_w2.astype(BF16)
    hg_in_all, hg_out_all = hg_w_in.astype(BF16), hg_w_out.astype(BF16)

    xp, xs = x_prompt, x_sample
    hp = hs = None
    l_ckv, l_kpe, l_s5r, l_s5i, l_hg = [], [], [], [], []
    for layer in range(depth):
        j = layer // 2
        m = mods[layer]
        ms = [m[:nsmp, i * D:(i + 1) * D][:, None, :] for i in range(6)]
        mp = [m[nsmp:nsmp + 1, i * D:(i + 1) * D][:, None, :] for i in range(6)]
        if layer == 0:
            hp = norm_mod(xp, norm_g[layer, 0], mp[1], mp[0])
            hs = norm_mod(xs, norm_g[layer, 0], ms[1], ms[0])
        if layer % 2 == 0:
            wts = _ab_weights(ab_w_in[j], ab_w_out[j], mla_q_norm[j], mla_kv_norm[j], mla_w_uq[j], mla_w_ukv[j],
                              s5w, q_rank, kv_rank, mla_heads)
            s5m = _s5_weights(s5_log_dt[j], s5_lam_re[j], s5_lam_im[j], s5_b_re[j], s5_b_im[j],
                              s5_c_re[j], s5_c_im[j], s5_d[j])
            nb = s5m[0].shape[0]
            zero_h0 = jnp.zeros((nb, xp.shape[0], 1, s5m[3].shape[-1]), F32)
            yp, ckv, kpe, fin = _ab_mixer(hp, wts, s5m, s5_glu_w[j], s5_glu_b[j], heads=mla_heads,
                                          kv_rank=kv_rank, ctx_ckv=None, ctx_kpe=None, h0=zero_h0, rope=False)
            h0s = _pack_s5_state(state_s5_re[:, j], state_s5_im[:, j], nb)
            ys, _, _, _ = _ab_mixer(hs, wts, s5m, s5_glu_w[j], s5_glu_b[j], heads=mla_heads, kv_rank=kv_rank,
                                    ctx_ckv=cache_ckv[:, j], ctx_kpe=cache_kpe[:, j], h0=h0s, rope=True)
            hr, hi = _unpack_s5_state(fin, groups, states)
            l_ckv.append(ckv.reshape(xp.shape[0], xp.shape[1], kv_rank))
            l_kpe.append(kpe.reshape(xp.shape[0], xp.shape[1], 64))
            l_s5r.append(hr)
            l_s5i.append(hi)
        else:
            yp, st = _hg_mixer(hp, hg_in_all, hg_out_all, j, lbs[:, layer], hg_out_norm[j], None, hg_heads)
            ys, _ = _hg_mixer(hs, hg_in_all, hg_out_all, j, lbs[:, layer], hg_out_norm[j], state_hgrn[:, j],
                              hg_heads)
            l_hg.append(st)
        xp, hp = resid_norm(xp, yp, norm_g[layer, 1], mp[2], (norm_g[layer, 2], mp[4], mp[3]))
        xs, hs = resid_norm(xs, ys, norm_g[layer, 1], ms[2], (norm_g[layer, 2], ms[4], ms[3]))
        yp = _mlp(hp, w1_all, w2_all, layer)
        ys = _mlp(hs, w1_all, w2_all, layer)
        if layer + 1 < depth:
            m_n = mods[layer + 1]
            nxt_s = (norm_g[layer + 1, 0], m_n[:nsmp, D:2 * D][:, None, :], m_n[:nsmp, 0:D][:, None, :])
            nxt_p = (norm_g[layer + 1, 0], m_n[nsmp:nsmp + 1, D:2 * D][:, None, :],
                     m_n[nsmp:nsmp + 1, 0:D][:, None, :])
        else:
            nxt_s = nxt_p = None
        xp, hp = resid_norm(xp, yp, norm_g[layer, 3], mp[5], nxt_p)
        xs, hs = resid_norm(xs, ys, norm_g[layer, 3], ms[5], nxt_s)
    new_ckv = jnp.stack(l_ckv, axis=1)
    new_kpe = jnp.stack(l_kpe, axis=1)
    new_s5_re = jnp.stack(l_s5r, axis=1)
    new_s5_im = jnp.stack(l_s5i, axis=1)
    new_hgrn = jnp.stack(l_hg, axis=1)
    return (xp, xs, new_ckv, new_kpe, new_s5_re, new_s5_im, new_hgrn)
```

```python
import functools
import math

import jax
import jax.numpy as jnp
from jax import lax
from jax.experimental import pallas as pl
from jax.experimental.pallas import tpu as pltpu

F32 = jnp.float32
BF16 = jnp.bfloat16

EPS = 1e-6
GRID_W = 64
ROPE_BASE = 10000.0
LANES = 128
S5_CHUNK = 16
HG_SUB = 32
HG_BLOCK = 2 * HG_SUB
VMEM_LIMIT = 56 * 1024 * 1024
MM_VMEM_BUDGET = 42 * 1024 * 1024
ATT_HEADS = 2
HG_POST_HEADS = 8
HG_HEADS = 4
HG_BLOCKS = 16
S5_SEQS = 4


def _pick(n, prefs):
    for p in prefs:
        if n % p == 0:
            return p
    return n


def _cparams(sem):
    return pltpu.CompilerParams(dimension_semantics=sem, vmem_limit_bytes=VMEM_LIMIT)


def _sigmoid(x):
    return 0.5 + 0.5 * jnp.tanh(0.5 * x)


def _silu(x):
    return x * _sigmoid(x)


def _gelu_tanh(x):
    c = math.sqrt(2.0 / math.pi)
    return 0.5 * x * (1.0 + jnp.tanh(c * (x + 0.044715 * (x * x * x))))


def _rms(x):
    return x * lax.rsqrt(jnp.mean(x * x, axis=-1, keepdims=True) + EPS)


def _mm_kernel(*refs, nk, n_aux, epilogue, a_blocked, out_mode, fold):
    a_ref, w_ref = refs[0], refs[1]
    aux_refs = refs[2:2 + n_aux]
    o_ref = refs[2 + n_aux]
    scratch = list(refs[3 + n_aux:])

    def load_a():
        if a_blocked:
            return jnp.concatenate([a_ref[c] for c in range(a_ref.shape[0])], axis=-1).astype(BF16)
        return a_ref[0].astype(BF16)

    def finish(acc):
        aux = []
        for r in aux_refs:
            v = r[...]
            aux.append(v)
        out = epilogue(acc, *aux) if epilogue is not None else acc
        if out_mode == "fold_rows":
            slab_ref = scratch.pop()
            rows = slab_ref.shape[1] // fold
            for c in range(o_ref.shape[0]):
                slab_ref[c] = out[:, c * LANES:(c + 1) * LANES]
            for c in range(o_ref.shape[0]):
                for t in range(fold):
                    o_ref[c, :, t * LANES:(t + 1) * LANES] = (
                        slab_ref[c, pl.ds(t, rows, stride=fold), :].astype(o_ref.dtype))
        elif out_mode == "unfold_rows":
            rows = out.shape[0]
            for t in range(fold):
                o_ref[0, pl.ds(t, rows, stride=fold), :] = out[:, t * LANES:(t + 1) * LANES].astype(o_ref.dtype)
        else:
            o_ref[0] = out.astype(o_ref.dtype)

    if nk == 1:
        finish(jnp.dot(load_a(), w_ref[0].astype(BF16), preferred_element_type=F32))
        return

    acc_ref = scratch.pop(0)
    k = pl.program_id(3)

    @pl.when(k == 0)
    def _():
        acc_ref[...] = jnp.zeros_like(acc_ref)

    acc_ref[...] += jnp.dot(load_a(), w_ref[0].astype(BF16), preferred_element_type=F32)

    @pl.when(k == nk - 1)
    def _():
        finish(acc_ref[...])


def _pick_tk(K, tm, tn, a_bytes, w_bytes, out_bytes, aux_bytes):
    for tk in (K, 4096, 2048, 1024, 512, 256, 128):
        if tk > K or K % tk:
            continue
        acc = 0 if tk == K else 4 * tm * tn
        est = 2 * (tm * tk * a_bytes + tk * tn * w_bytes) + 2 * tm * tn * out_bytes + acc + 2 * aux_bytes
        if est <= MM_VMEM_BUDGET:
            return tk
    return LANES


def mm(a, w, *, wg=0, epilogue=None, aux=(), out_dtype=F32, tm=None, tn=None, tk=None,
       a_blocked=False, out_mode="plain", fold=1, name="mm"):
    _, K, N = w.shape
    G = 1 if a_blocked else a.shape[0]
    M = a.shape[1]
    tm = tm or _pick(M, (1024, 512, 256, 128))
    tn = tn or _pick(N, (1024, 512, 256, 128))
    if tk is None:
        aux_bytes = sum(math.prod(bshape) * arr.dtype.itemsize for arr, bshape, _ in aux)
        tk = _pick_tk(K, tm, tn, a.dtype.itemsize, w.dtype.itemsize, jnp.dtype(out_dtype).itemsize, aux_bytes)
    assert M % tm == 0 and N % tn == 0 and K % tk == 0, (M, N, K, tm, tn, tk)
    nk = K // tk
    grid = (G, M // tm, N // tn, nk)
    if a_blocked:
        a_spec = pl.BlockSpec((tk // LANES, tm, LANES), lambda g, i, j, k: (k, i, 0))
    else:
        a_spec = pl.BlockSpec((1, tm, tk), lambda g, i, j, k: (g, i, k))
    w_spec = pl.BlockSpec((1, tk, tn), lambda g, i, j, k: (wg + g, k, j))
    aux_arrays, aux_specs = [], []
    for arr, bshape, imap in aux:
        aux_arrays.append(arr)
        aux_specs.append(pl.BlockSpec(bshape, functools.partial(lambda g, i, j, k, f: f(g, i, j), f=imap)))
    scratch = [] if nk == 1 else [pltpu.VMEM((tm, tn), F32)]
    if out_mode == "fold_rows":
        assert G == 1 and tm % fold == 0
        out_shape = jax.ShapeDtypeStruct((N // LANES, M // fold, fold * LANES), out_dtype)
        out_spec = pl.BlockSpec((tn // LANES, tm // fold, fold * LANES), lambda g, i, j, k: (j, i, 0))
        scratch.append(pltpu.VMEM((tn // LANES, tm, LANES), F32))
    elif out_mode == "unfold_rows":
        assert tn == N == fold * LANES
        out_shape = jax.ShapeDtypeStruct((G, M * fold, LANES), out_dtype)
        out_spec = pl.BlockSpec((1, tm * fold, LANES), lambda g, i, j, k: (g, i, 0))
    else:
        out_shape = jax.ShapeDtypeStruct((G, M, N), out_dtype)
        out_spec = pl.BlockSpec((1, tm, tn), lambda g, i, j, k: (g, i, j))
    kern = functools.partial(_mm_kernel, nk=nk, n_aux=len(aux_arrays), epilogue=epilogue,
                             a_blocked=a_blocked, out_mode=out_mode, fold=fold)
    return pl.pallas_call(
        kern, grid=grid, in_specs=[a_spec, w_spec] + aux_specs, out_specs=out_spec,
        out_shape=out_shape, scratch_shapes=scratch, name=name,
        compiler_params=_cparams(("parallel", "parallel", "parallel", "arbitrary")),
    )(a, w, *aux_arrays)


def mm2d(a, w, **kw):
    out = mm(a[None], w if w.ndim == 3 else w[None], **kw)
    return out if kw.get("out_mode", "plain") != "plain" else out[0]


def _epi_relu2(acc):
    r = jnp.maximum(acc, 0.0)
    return r * r


def _epi_rms(acc, g):
    return _rms(acc) * g


def _epi_ckv(acc, g, *, rank):
    return jnp.concatenate([_rms(acc[:, :rank]) * g, acc[:, rank:]], axis=-1)


def _epi_table(acc, t):
    reps = acc.shape[1] // t.shape[1]
    return acc * jnp.concatenate([t] * reps, axis=-1)


def _epi_kadd(acc, kpks, t):
    kr2 = kpks[:, :LANES] * t[:, :LANES] + kpks[:, LANES:] * t[:, LANES:]
    blk = jnp.concatenate([jnp.zeros_like(kr2), kr2], axis=-1)
    reps = acc.shape[1] // blk.shape[1]
    return acc + jnp.concatenate([blk] * reps, axis=-1)


def _epi_add_gelu(acc, y0):
    return _gelu_tanh(acc + y0[0])


def _epi_glu(acc, gy, b):
    y = jnp.concatenate([gy[c] for c in range(gy.shape[0])], axis=-1).astype(F32)
    return y * _sigmoid(acc + b)


def _mod_kernel(c_ref, w_ref, b_ref, o_ref, *, nk):
    k = pl.program_id(1)

    @pl.when(k == 0)
    def _():
        o_ref[...] = jnp.zeros_like(o_ref)

    a = _silu(c_ref[...]).astype(BF16)
    o_ref[...] += jnp.dot(a, w_ref[...].astype(BF16), preferred_element_type=F32)

    @pl.when(k == nk - 1)
    def _():
        o_ref[...] += b_ref[...]


def modulation(cond, w, b, layer):
    rows, d = cond.shape
    n = w.shape[2]
    tn = _pick(n, (2048, 1024, 512, 256, 128))
    tk = _pick(d, (1024, 512, 256, 128))
    nk = d // tk
    return pl.pallas_call(
        functools.partial(_mod_kernel, nk=nk), grid=(n // tn, nk),
        in_specs=[pl.BlockSpec((rows, tk), lambda j, k: (0, k)),
                  pl.BlockSpec((None, tk, tn), lambda j, k: (layer, k, j)),
                  pl.BlockSpec((None, 1, tn), lambda j, k: (layer, 0, j))],
        out_specs=pl.BlockSpec((rows, tn), lambda j, k: (0, j)),
        out_shape=jax.ShapeDtypeStruct((rows, n), F32), name="modulation",
        compiler_params=_cparams(("parallel", "arbitrary")),
    )(cond, w, b[:, None, :])


def _norm_mod_kernel(x_ref, g_ref, sc_ref, sh_ref, o_ref):
    y = _rms(x_ref[0]) * g_ref[...]
    o_ref[0] = (y * (1.0 + sc_ref[0]) + sh_ref[0]).astype(o_ref.dtype)


def norm_mod(x, g, sc, sh):
    nseq, L, D = x.shape
    tr = _pick(L, (256, 128))
    per_seq = sc.shape[0] == nseq and nseq > 1
    smap = (lambda b, i: (b, 0, 0)) if per_seq else (lambda b, i: (0, 0, 0))
    return pl.pallas_call(
        _norm_mod_kernel, grid=(nseq, L // tr),
        in_specs=[pl.BlockSpec((1, tr, D), lambda b, i: (b, i, 0)),
                  pl.BlockSpec((1, D), lambda b, i: (0, 0)),
                  pl.BlockSpec((1, 1, D), smap), pl.BlockSpec((1, 1, D), smap)],
        out_specs=pl.BlockSpec((1, tr, D), lambda b, i: (b, i, 0)),
        out_shape=jax.ShapeDtypeStruct((nseq, L, D), BF16), name="norm_mod",
        compiler_params=_cparams(("parallel", "parallel")),
    )(x, g[None], sc, sh)


def _resid_kernel(x_ref, y_ref, g1_ref, gt_ref, *rest, with_h):
    xn = x_ref[0] + gt_ref[0] * (_rms(y_ref[0].astype(F32)) * g1_ref[...])
    if with_h:
        g2_ref, sc_ref, sh_ref, xo_ref, h_ref = rest
        xo_ref[0] = xn
        h_ref[0] = ((_rms(xn) * g2_ref[...]) * (1.0 + sc_ref[0]) + sh_ref[0]).astype(h_ref.dtype)
    else:
        (xo_ref,) = rest
        xo_ref[0] = xn


def resid_norm(x, y, g1, gate, nxt=None):
    nseq, L, D = x.shape
    tr = _pick(L, (256, 128))
    per_seq = gate.shape[0] == nseq and nseq > 1
    smap = (lambda b, i: (b, 0, 0)) if per_seq else (lambda b, i: (0, 0, 0))
    row = pl.BlockSpec((1, tr, D), lambda b, i: (b, i, 0))
    vec = pl.BlockSpec((1, D), lambda b, i: (0, 0))
    mod = pl.BlockSpec((1, 1, D), smap)
    args = [x, y, g1[None], gate]
    specs = [row, row, vec, mod]
    out_shape = [jax.ShapeDtypeStruct((nseq, L, D), F32)]
    out_specs = [row]
    if nxt is not None:
        g2, sc, sh = nxt
        args += [g2[None], sc, sh]
        specs += [vec, mod, mod]
        out_shape.append(jax.ShapeDtypeStruct((nseq, L, D), BF16))
        out_specs.append(row)
    res = pl.pallas_call(
        functools.partial(_resid_kernel, with_h=nxt is not None), grid=(nseq, L // tr),
        in_specs=specs, out_specs=out_specs, out_shape=out_shape, name="resid_norm",
        compiler_params=_cparams(("parallel", "parallel")),
    )(*args)
    return (res[0], res[1]) if nxt is not None else (res[0], None)


def _attn_kernel(q_ref, k_ref, v_ref, o_ref, *, hp):
    scores = []
    for h in range(hp):
        qk = slice(h * 2 * LANES, (h + 1) * 2 * LANES)
        scores.append(lax.dot_general(q_ref[0, :, qk], k_ref[0, :, qk], (((1,), (1,)), ((), ())),
                                      preferred_element_type=F32))
    probs = []
    for s in scores:
        m = jnp.max(s, axis=-1, keepdims=True)
        p = jnp.exp2(s - m)
        probs.append((p.astype(BF16), jnp.sum(p, axis=-1, keepdims=True)))
    for h, (p, l) in enumerate(probs):
        vo = slice(h * LANES, (h + 1) * LANES)
        o = jnp.dot(p, v_ref[0, :, vo], preferred_element_type=F32)
        o_ref[0, :, vo] = (o / l).astype(o_ref.dtype)


def attention(q, kcat, v, heads):
    nseq, L, _ = q.shape
    Lk = kcat.shape[1]
    tq = _pick(L, (256, 128))
    hp = ATT_HEADS if heads % ATT_HEADS == 0 else 1
    return pl.pallas_call(
        functools.partial(_attn_kernel, hp=hp), grid=(nseq, heads // hp, L // tq),
        in_specs=[pl.BlockSpec((1, tq, hp * 2 * LANES), lambda b, h, i: (b, i, h)),
                  pl.BlockSpec((1, Lk, hp * 2 * LANES), lambda b, h, i: (b, 0, h)),
                  pl.BlockSpec((1, Lk, hp * LANES), lambda b, h, i: (b, 0, h))],
        out_specs=pl.BlockSpec((1, tq, hp * LANES), lambda b, h, i: (b, i, h)),
        out_shape=jax.ShapeDtypeStruct((nseq, L, heads * LANES), BF16), name="mla_attention",
        compiler_params=_cparams(("parallel", "parallel", "arbitrary")),
    )(q, kcat, v)


def _s5_scan_kernel(g_ref, ad_ref, h0_ref, sin_ref, fin_ref, *, n, sb, half):
    ad = ad_ref[0]
    afr, afi = ad[:, 0:half], ad[:, half:2 * half]
    abr, abi = ad[:, 2 * half:3 * half], ad[:, 3 * half:4 * half]

    def body(k, carry):
        out = []
        for s in range(sb):
            fr, fi, br, bi = carry[4 * s:4 * s + 4]
            row_f = pl.ds(s * n + k, 1)
            row_b = pl.ds(s * n + n - 1 - k, 1)
            sin_ref[0, row_f, 0:2 * half] = jnp.concatenate([fr, fi], axis=-1)
            sin_ref[0, row_b, 2 * half:4 * half] = jnp.concatenate([br, bi], axis=-1)
            gf = g_ref[0, row_f, 0:2 * half]
            gb = g_ref[0, row_b, 2 * half:4 * half]
            out += [afr * fr - afi * fi + gf[:, :half], afr * fi + afi * fr + gf[:, half:],
                    abr * br - abi * bi + gb[:, :half], abr * bi + abi * br + gb[:, half:]]
        return tuple(out)

    init = []
    for s in range(sb):
        h0 = h0_ref[0, s]
        init += [h0[:, 0:half], h0[:, half:2 * half], h0[:, 2 * half:3 * half], h0[:, 3 * half:4 * half]]
    fin = lax.fori_loop(0, n, body, tuple(init))
    for s in range(sb):
        fin_ref[0, s] = jnp.concatenate(fin[4 * s:4 * s + 4], axis=-1)


def s5_scan(g, ad, h0, n):
    nb, rows, W = g.shape
    nseq = rows // n
    sb = S5_SEQS if nseq % S5_SEQS == 0 else 1
    return pl.pallas_call(
        functools.partial(_s5_scan_kernel, n=n, sb=sb, half=W // 4), grid=(nb, nseq // sb),
        in_specs=[pl.BlockSpec((1, sb * n, W), lambda j, b: (j, b, 0)),
                  pl.BlockSpec((1, 1, W), lambda j, b: (j, 0, 0)),
                  pl.BlockSpec((1, sb, 1, W), lambda j, b: (j, b, 0, 0))],
        out_specs=[pl.BlockSpec((1, sb * n, W), lambda j, b: (j, b, 0)),
                   pl.BlockSpec((1, sb, 1, W), lambda j, b: (j, b, 0, 0))],
        out_shape=[jax.ShapeDtypeStruct((nb, rows, W), F32),
                   jax.ShapeDtypeStruct((nb, nseq, 1, W), F32)], name="s5_scan",
        compiler_params=_cparams(("parallel", "parallel")),
    )(g, ad, h0)


def _hg_masks(rev):
    T = HG_BLOCK
    row = lax.broadcasted_iota(jnp.int32, (T, LANES), 0)
    first = row < HG_SUB
    r2 = lax.broadcasted_iota(jnp.int32, (T, 2 * T), 0)
    c2 = lax.broadcasted_iota(jnp.int32, (T, 2 * T), 1)
    s2 = c2 & (T - 1)
    sub_bits = HG_SUB.bit_length() - 1
    other_sub = (r2 ^ s2) >> sub_bits
    causal = (s2 >= r2) if rev else (s2 <= r2)
    keep = jnp.where(other_sub == (c2 >> (sub_bits + 1)), jnp.where(causal, 1, 0), 0) > 0
    rin = row & (HG_SUB - 1)
    steps = (1, 2, 4, 8, 16)
    scan = [(rin < HG_SUB - s) if rev else (rin >= s) for s in steps]
    return steps, scan, first, keep


def _hg_direction(q, z, v, lb, st, rev, masks):
    T = HG_BLOCK
    steps, scan, first, keep = masks
    qa = _silu(q) * (LANES ** -0.5)
    th = 0.5 * jnp.tanh(0.5 * z)
    sig = 0.5 + th
    nsig = 0.5 - th
    kk = (1.0 - lb) * nsig
    g = jnp.log(lb + (1.0 - lb) * sig)

    b = g
    for s, ok in zip(steps, scan):
        b = b + jnp.where(ok, pltpu.roll(b, (T - s) if rev else s, axis=0), 0.0)
    if not rev:
        b0, b1 = b[HG_SUB - 1:HG_SUB], b[T - 1:T]
    else:
        b0, b1 = b[0:1], b[HG_SUB:HG_SUB + 1]
    bsub = jnp.where(first, b0, b1)
    qh = qa * jnp.exp(b)
    kd = kk * jnp.exp(-b)
    ke = kk * jnp.exp(bsub - b)

    kcat = jnp.concatenate([kd, ke], axis=0).astype(BF16)
    att = lax.dot_general(qh.astype(BF16), kcat, (((1,), (1,)), ((), ())), preferred_element_type=F32)
    att = jnp.where(keep, att, 0.0)
    vb = v.astype(BF16)
    o = jnp.dot(att.astype(BF16), jnp.concatenate([vb, vb], axis=0), preferred_element_type=F32)

    if not rev:
        dq = jnp.where(first, 1.0, jnp.exp(b0))
        ek = jnp.where(first, jnp.exp(b1), 1.0)
    else:
        dq = jnp.where(first, jnp.exp(b1), 1.0)
        ek = jnp.where(first, 1.0, jnp.exp(b0))
    o = o + lax.dot_general((qh * dq).astype(BF16), st.astype(BF16), (((1,), (1,)), ((), ())),
                            preferred_element_type=F32)
    upd = lax.dot_general(vb, (ke * ek).astype(BF16), (((0,), (0,)), ((), ())), preferred_element_type=F32)
    st_new = st * jnp.exp(b0 + b1) + upd
    return o, st_new


def _hg_kernel(*refs, nsteps, nb, hp, has_init):
    if has_init:
        (qf_ref, zf_ref, vf_ref, qb_ref, zb_ref, vb_ref, lb_ref, s0_ref,
         of_ref, ob_ref, so_ref, stf_ref, stb_ref) = refs
    else:
        (qf_ref, zf_ref, vf_ref, qb_ref, zb_ref, vb_ref, lb_ref,
         of_ref, ob_ref, so_ref, stf_ref, stb_ref) = refs
    i = pl.program_id(2)
    T = HG_BLOCK

    @pl.when(i == 0)
    def _():
        for h in range(hp):
            if has_init:
                stf_ref[h] = s0_ref[0, 0, h].T
                stb_ref[h] = s0_ref[0, 1, h].T
            else:
                stf_ref[h] = jnp.zeros((LANES, LANES), F32)
                stb_ref[h] = jnp.zeros((LANES, LANES), F32)

    lb = lb_ref[...]
    masks_f = _hg_masks(False)
    masks_b = _hg_masks(True)
    for h in range(hp):
        lanes = slice(h * LANES, (h + 1) * LANES)
        st_f = stf_ref[h]
        st_b = stb_ref[h]
        for blk in range(nb):
            rows = slice(blk * T, (blk + 1) * T)
            o_f, st_f = _hg_direction(qf_ref[0, rows, lanes], zf_ref[0, rows, lanes], vf_ref[0, rows, lanes],
                                      lb[0:1, lanes], st_f, False, masks_f)
            of_ref[0, rows, lanes] = o_f
        for blk in reversed(range(nb)):
            rows = slice(blk * T, (blk + 1) * T)
            o_b, st_b = _hg_direction(qb_ref[0, rows, lanes], zb_ref[0, rows, lanes], vb_ref[0, rows, lanes],
                                      lb[1:2, lanes], st_b, True, masks_b)
            ob_ref[0, rows, lanes] = o_b
        stf_ref[h] = st_f
        stb_ref[h] = st_b

        @pl.when(i == nsteps - 1)
        def _(h=h, st_f=st_f, st_b=st_b):
            so_ref[0, 0, h] = st_f.T
            so_ref[0, 1, h] = st_b.T


def hgrn2(z, lb, s0, heads):
    nseq, L, _ = z.shape
    hp = HG_HEADS if heads % HG_HEADS == 0 else 1
    nb = max(n for n in range(1, HG_BLOCKS + 1) if L % (n * HG_BLOCK) == 0)
    T = nb * HG_BLOCK
    nsteps = L // T
    HB = heads // hp

    def col(off, rev):
        if rev:
            return lambda b, h, i: (b, nsteps - 1 - i, off + h)
        return lambda b, h, i: (b, i, off + h)

    tile = lambda off, rev: pl.BlockSpec((1, T, hp * LANES), col(off, rev))
    in_specs = [tile(0, False), tile(HB, False), tile(3 * HB, False),
                tile(0, True), tile(2 * HB, True), tile(3 * HB, True),
                pl.BlockSpec((2, hp * LANES), lambda b, h, i: (0, h))]
    args = [z, z, z, z, z, z, lb]
    st_spec = pl.BlockSpec((1, 2, hp, LANES, LANES), lambda b, h, i: (b, 0, h, 0, 0))
    if s0 is not None:
        in_specs.append(st_spec)
        args.append(s0)
    o_shape = jax.ShapeDtypeStruct((nseq, L, heads * LANES), F32)
    st_scratch = pltpu.VMEM((hp, LANES, LANES), F32)
    return pl.pallas_call(
        functools.partial(_hg_kernel, nsteps=nsteps, nb=nb, hp=hp, has_init=s0 is not None),
        grid=(nseq, HB, nsteps), in_specs=in_specs,
        out_specs=[tile(0, False), tile(0, True), st_spec],
        out_shape=[o_shape, o_shape, jax.ShapeDtypeStruct((nseq, 2, heads, LANES, LANES), F32)],
        scratch_shapes=[st_scratch, st_scratch], name="hgrn2",
        compiler_params=_cparams(("parallel", "parallel", "arbitrary")),
    )(*args)


def _hg_post_kernel(of_ref, ob_ref, g_ref, gn_ref, o_ref, *, hp):
    for h in range(hp):
        lanes = slice(h * LANES, (h + 1) * LANES)
        o = of_ref[0, :, lanes] + ob_ref[0, :, lanes]
        o_ref[0, :, lanes] = ((_rms(o) * gn_ref[:, lanes]) * _silu(g_ref[0, :, lanes])).astype(o_ref.dtype)


def hg_post(o_f, o_b, z, out_norm, heads):
    nseq, L, _ = o_f.shape
    tr = _pick(L, (256, 128))
    hp = HG_POST_HEADS if heads % HG_POST_HEADS == 0 else 1
    tile = pl.BlockSpec((1, tr, hp * LANES), lambda b, i, h: (b, i, h))
    return pl.pallas_call(
        functools.partial(_hg_post_kernel, hp=hp), grid=(nseq, L // tr, heads // hp),
        in_specs=[tile, tile, pl.BlockSpec((1, tr, hp * LANES), lambda b, i, h: (b, i, 4 * (heads // hp) + h)),
                  pl.BlockSpec((1, hp * LANES), lambda b, i, h: (0, h))],
        out_specs=tile, out_shape=jax.ShapeDtypeStruct(o_f.shape, BF16), name="hg_post",
        compiler_params=_cparams(("parallel", "parallel", "parallel")),
    )(o_f, o_b, z, out_norm[None])


def _rope_tables(n_l):
    rope = 64
    half = rope // 2
    rows = n_l // GRID_W
    row = jnp.repeat(jnp.arange(rows), GRID_W).astype(F32)
    col = jnp.tile(jnp.arange(GRID_W), rows).astype(F32)
    inv = ROPE_BASE ** (-jnp.arange(0, half, 2, dtype=F32) / half)
    ar = row[:, None] * inv[None]
    ac = col[:, None] * inv[None]
    cos = jnp.concatenate([jnp.cos(ar), jnp.cos(ar), jnp.cos(ac), jnp.cos(ac)], axis=-1)
    sin = jnp.concatenate([jnp.sin(ar), jnp.sin(ar), jnp.sin(ac), jnp.sin(ac)], axis=-1)
    return cos, sin


def _rot_cols(w):
    return jnp.concatenate([-w[..., 16:32], w[..., 0:16], -w[..., 48:64], w[..., 32:48]], axis=-1)


def _s5_weights(log_dt, lam_re, lam_im, b_re, b_im, c_re, c_im, d_skip):
    D = S5_CHUNK
    G, P = lam_re.shape[1], lam_re.shape[2]
    C = b_re.shape[-1]
    gl = LANES // C
    nb = G // gl
    dt = jnp.exp(log_dt.astype(F32))[..., None]
    lr, li = lam_re.astype(F32), lam_im.astype(F32)
    mag = jnp.exp(lr * dt)
    ar, ai = mag * jnp.cos(li * dt), mag * jnp.sin(li * dt)
    den = lr * lr + li * li
    cr_ = ((ar - 1.0) * lr + ai * li) / den
    ci_ = (ai * lr - (ar - 1.0) * li) / den
    br, bi = b_re.astype(F32), b_im.astype(F32)
    bbr = cr_[..., None] * br - ci_[..., None] * bi
    bbi = cr_[..., None] * bi + ci_[..., None] * br
    cr, ci = c_re.astype(F32), c_im.astype(F32)
    def powers(n):
        n = n.astype(F32)[:, None, None, None]
        pmag = jnp.exp(n * (lr * dt)[None])
        return pmag * jnp.cos(n * (li * dt)[None]), pmag * jnp.sin(n * (li * dt)[None])

    def times_b(p_r, p_i):
        return (p_r[..., None] * bbr[None] - p_i[..., None] * bbi[None],
                p_r[..., None] * bbi[None] + p_i[..., None] * bbr[None])

    steps = jnp.arange(D)
    pr, pi = powers(jnp.arange(D + 1))
    abr, abi = times_b(pr[:D], pi[:D])
    abr_dn, abi_dn = times_b(*powers(D - 1 - steps))
    pr_dn, pi_dn = powers(D - steps)
    half = gl * P
    lane_group = jnp.arange(LANES) // C
    tok_lane_group = jnp.tile(lane_group, D)
    state_group = jnp.arange(half) // P

    def response(x, a_r, a_i):
        return jnp.einsum('gcp,ngpk->ngkc', cr[x], a_r) - jnp.einsum('gcp,ngpk->ngkc', ci[x], a_i)

    kf = response(0, abr[:, 0], abi[:, 0])
    kb = response(1, abr_dn[:, 1], abi_dn[:, 1])
    skip = d_skip.astype(F32)[:, :, None] * jnp.eye(C, dtype=F32)[None]
    zpad = jnp.zeros((D - 1,) + kf.shape[1:], F32)
    lagk = (jnp.concatenate([zpad, kf], axis=0) + jnp.concatenate([kb, zpad], axis=0)
            + jnp.concatenate([zpad, skip[None], zpad], axis=0))
    bd = jnp.tile(lagk.reshape(2 * D - 1, nb, LANES, C), (1, 1, 1, gl))
    bd = jnp.where(lane_group[:, None] == lane_group[None, :], bd, 0.0).astype(BF16)
    lag_idx = jnp.arange(D)[None, :] - jnp.arange(D)[:, None] + (D - 1)
    w_t = bd[lag_idx].transpose(2, 0, 3, 1, 4).reshape(nb, D * LANES, D * LANES)

    def to_state(x):
        rows = x.reshape(D, nb, gl, P, C).transpose(1, 0, 2, 4, 3).reshape(nb, D * LANES, P)
        tiled = jnp.tile(rows, (1, 1, gl))
        return jnp.where(tok_lane_group[:, None] == state_group[None, :], tiled, 0.0).astype(BF16)

    w_b = jnp.concatenate([to_state(abr_dn[:, 0]), to_state(abi_dn[:, 0]),
                           to_state(abr[:, 1]), to_state(abi[:, 1])], axis=-1)

    def from_state(y):
        cols = y.reshape(D, nb, gl, C, P).transpose(1, 4, 0, 2, 3).reshape(nb, P, D * LANES)
        tiled = jnp.tile(cols, (1, gl, 1))
        return jnp.where(state_group[:, None] == tok_lane_group[None, :], tiled, 0.0).astype(BF16)

    def c_times(x, p_r, p_i):
        re = cr[x][None] * p_r[:, :, None, :] - ci[x][None] * p_i[:, :, None, :]
        im = cr[x][None] * p_i[:, :, None, :] + ci[x][None] * p_r[:, :, None, :]
        return re, -im

    cf = c_times(0, pr[1:D + 1, 0], pi[1:D + 1, 0])
    cb = c_times(1, pr_dn[:, 1], pi_dn[:, 1])
    w_c = jnp.concatenate([from_state(cf[0]), from_state(cf[1]), from_state(cb[0]), from_state(cb[1])], axis=1)
    adr = pr[D].reshape(2, nb, half)
    adi = pi[D].reshape(2, nb, half)
    ad = jnp.concatenate([adr[0], adi[0], adr[1], adi[1]], axis=-1)[:, None, :]
    return w_t, w_b, w_c, ad


def _ab_weights(w_in, w_out, q_norm, kv_norm, w_uq, w_ukv, s5w, q_rank, kv_rank, heads):
    rope, nope, vdim = 64, 128, 128
    o1, o2, o3 = s5w, s5w + q_rank, s5w + q_rank + kv_rank
    w_u = w_in[:, :o1].astype(BF16)
    w_q = w_in[:, o1:o2].astype(BF16)
    w_kp = w_in[:, o3:]
    w_ks = _rot_cols(w_kp)
    w_small = jnp.concatenate([w_in[:, o2:o3], w_kp, w_kp, w_ks, w_ks], axis=-1).astype(BF16)
    uq = w_uq.reshape(q_rank, heads, nope + rope)
    uq_aug = jnp.concatenate([uq[..., :nope], uq[..., nope:], _rot_cols(uq[..., nope:])], axis=-1)
    uq_aug = uq_aug.reshape(q_rank, heads * 2 * LANES).astype(BF16)
    ukv = w_ukv.reshape(kv_rank, heads, nope + vdim)
    ukn_aug = jnp.concatenate([ukv[..., :nope], jnp.zeros_like(ukv[..., :nope])], axis=-1)
    ukn_aug = ukn_aug.reshape(kv_rank, heads * 2 * LANES).astype(BF16)
    uv = ukv[..., nope:].reshape(kv_rank, heads * vdim).astype(BF16)
    return dict(w_u=w_u, w_q=w_q, w_small=w_small, uq_aug=uq_aug, ukn_aug=ukn_aug, uv=uv,
                w_out=w_out.astype(BF16), q_norm=q_norm[None], kv_norm=kv_norm[None])


def _ab_mixer(h, wts, s5m, glu_w, glu_b, *, heads, kv_rank, ctx_ckv, ctx_kpe, h0, rope):
    nseq, L, D = h.shape
    M = nseq * L
    h2 = h.reshape(M, D)
    w_t, w_b, w_c, ad = s5m
    nb = w_t.shape[0]
    scale = (128 + 64) ** -0.5 * math.log2(math.e)

    u2 = mm2d(h2, wts['w_u'], out_dtype=BF16, out_mode="fold_rows", fold=S5_CHUNK, name="ab_in_u")
    tmq = _pick(L, (1024, 512, 256, 128))
    qlat = mm2d(h2, wts['w_q'], epilogue=_epi_rms, tn=wts['w_q'].shape[1], tm=_pick(M, (512, 256, 128)),
                aux=[(wts['q_norm'], (1, wts['w_q'].shape[1]), lambda g, i, j: (0, 0))],
                out_dtype=BF16, name="ab_in_q")
    nsm = wts['w_small'].shape[1]
    small = mm2d(h2, wts['w_small'], epilogue=functools.partial(_epi_ckv, rank=kv_rank), tn=nsm,
                 tm=_pick(M, (512, 256, 128)),
                 aux=[(wts['kv_norm'], (1, kv_rank), lambda g, i, j: (0, 0))], name="ab_in_kv")
    ckv = small[:, :kv_rank]
    kpks = small[:, kv_rank:]

    ones = jnp.ones((tmq, 64), F32)
    zeros = jnp.zeros((tmq, 64), F32)
    if rope:
        cos, sin = _rope_tables(L)
    else:
        cos, sin = ones, zeros
    one128 = jnp.ones((cos.shape[0], LANES), F32)
    tq = jnp.concatenate([one128, cos, sin], axis=-1) * scale
    nq_t = tq.shape[0] // tmq
    q = mm2d(qlat, wts['uq_aug'], epilogue=_epi_table, tm=tmq,
             aux=[(tq, (tmq, 2 * LANES), lambda g, i, j: (i % nq_t, 0))], out_dtype=BF16, name="mla_uq")

    if ctx_ckv is not None:
        past = ctx_ckv.shape[1]
        ckv_all = jnp.concatenate([ctx_ckv.astype(BF16), ckv.reshape(nseq, L, kv_rank).astype(BF16)], axis=1)
        ck = ctx_kpe.astype(F32)
        ctx_kp = jnp.concatenate([ck, ck, jnp.zeros_like(ck), jnp.zeros_like(ck)], axis=-1)
        kpks_all = jnp.concatenate([ctx_kp, kpks.reshape(nseq, L, 2 * LANES)], axis=1)
        tk_ctx = jnp.concatenate([jnp.ones((past, LANES), F32), jnp.zeros((past, LANES), F32)], axis=-1)
        tk_all = jnp.concatenate([tk_ctx, jnp.concatenate([cos, cos, sin, sin], axis=-1)], axis=0)
        Lk = past + L
    else:
        ckv_all = ckv.reshape(nseq, L, kv_rank).astype(BF16)
        kpks_all = kpks.reshape(nseq, L, 2 * LANES)
        Lk = L
        tk_all = None
    tmk = _pick(Lk, (512, 256, 128))
    if tk_all is None:
        tk_all = jnp.concatenate([jnp.ones((tmk, LANES), F32), jnp.zeros((tmk, LANES), F32)], axis=-1)
    nk_t = tk_all.shape[0] // tmk
    Mk = nseq * Lk
    ckv_all = ckv_all.reshape(Mk, kv_rank)
    kcat = mm2d(ckv_all, wts['ukn_aug'], epilogue=_epi_kadd, tm=tmk,
                aux=[(kpks_all.reshape(Mk, 2 * LANES), (tmk, 2 * LANES), lambda g, i, j: (i, 0)),
                     (tk_all, (tmk, 2 * LANES), lambda g, i, j: (i % nk_t, 0))],
                out_dtype=BF16, name="mla_ukn")
    v = mm2d(ckv_all, wts['uv'], tm=tmk, out_dtype=BF16, name="mla_uv")
    att = attention(q.reshape(nseq, L, -1), kcat.reshape(nseq, Lk, -1), v.reshape(nseq, Lk, -1), heads)

    Dc = S5_CHUNK
    R = M // Dc
    n = L // Dc
    y_intra = mm(u2, w_t, name="s5_intra")
    g = mm(u2, w_b, name="s5_to_state")
    s_in, fin = s5_scan(g, ad, h0, n)
    tms = _pick(R, (512, 256, 128))
    tns = Dc * LANES
    gyb = mm(s_in, w_c, epilogue=_epi_add_gelu, tm=tms, tn=tns,
             aux=[(y_intra, (1, tms, tns), lambda g_, i, j: (g_, i, j))], out_mode="unfold_rows", fold=Dc,
             name="s5_from_state")
    s5w = nb * LANES
    tmg = _pick(M, (1024, 512, 256, 128))
    tng = _pick(s5w, (1024, 512, 256, 128))
    s5_out = mm(gyb, glu_w.astype(BF16)[None], epilogue=_epi_glu, a_blocked=True, tm=tmg, tn=tng,
                aux=[(gyb, (tng // LANES, tmg, LANES), lambda g_, i, j: (j, i, 0)),
                     (glu_b[None], (1, tng), lambda g_, i, j: (0, j))], out_dtype=BF16, name="s5_glu")[0]

    cat = jnp.concatenate([s5_out, att.reshape(M, -1)], axis=-1)
    y = mm2d(cat, wts['w_out'], out_dtype=BF16, name="ab_out")
    return y.reshape(nseq, L, D), ckv, kpks[:, :64], fin


def _pack_s5_state(re, im, nb):
    nseq = re.shape[0]
    def blk(x):
        return x.reshape(nseq, nb, -1).transpose(1, 0, 2)
    parts = [blk(re[:, 0]), blk(im[:, 0]), blk(re[:, 1]), blk(im[:, 1])]
    return jnp.concatenate(parts, axis=-1)[:, :, None, :].astype(F32)


def _unpack_s5_state(fin, groups, states):
    nb, nseq = fin.shape[0], fin.shape[1]
    half = fin.shape[-1] // 4
    def blk(x):
        return x.transpose(1, 0, 2).reshape(nseq, groups, states)
    f = fin[:, :, 0]
    re = jnp.stack([blk(f[..., 0:half]), blk(f[..., 2 * half:3 * half])], axis=1)
    im = jnp.stack([blk(f[..., half:2 * half]), blk(f[..., 3 * half:])], axis=1)
    return re, im


def _hg_mixer(h, w_in, w_out, j, lb, out_norm, s0, heads):
    nseq, L, D = h.shape
    M = nseq * L
    z = mm2d(h.reshape(M, D), w_in, wg=j, name="hg_in").reshape(nseq, L, -1)
    o_f, o_b, st = hgrn2(z, lb, s0, heads)
    o = hg_post(o_f, o_b, z, out_norm, heads)
    y = mm2d(o.reshape(M, -1), w_out, wg=j, out_dtype=BF16, name="hg_out")
    return y.reshape(nseq, L, D), st


def _mlp(h, w1, w2, layer):
    nseq, L, D = h.shape
    M = nseq * L
    z = mm2d(h.reshape(M, D), w1, wg=layer, epilogue=_epi_relu2, out_dtype=BF16, name="mlp_up")
    return mm2d(z, w2, wg=layer, out_dtype=BF16, name="mlp_down").reshape(nseq, L, D)


def kernel(x_prompt, x_sample, cache_ckv, cache_kpe, state_s5_re, state_s5_im, state_hgrn, c, c_ctx,
           mod_w, mod_b, norm_g, mlp_w1, mlp_w2, ab_w_in, ab_w_out, mla_q_norm, mla_kv_norm, mla_w_uq,
           mla_w_ukv, s5_log_dt, s5_lam_re, s5_lam_im, s5_b_re, s5_b_im, s5_c_re, s5_c_im, s5_d, s5_glu_w,
           s5_glu_b, hg_w_in, hg_w_out, hg_lower_bounds, hg_out_norm):
    depth = mod_w.shape[0]
    D = x_prompt.shape[-1]
    nsmp = x_sample.shape[0]
    s5w = s5_glu_w.shape[-1]
    q_rank = mla_q_norm.shape[-1]
    kv_rank = mla_kv_norm.shape[-1]
    mla_heads = (D - s5w) // 128
    hg_heads = hg_out_norm.shape[-1] // 128
    groups, states = s5_lam_re.shape[2], s5_lam_re.shape[3]

    lbs = jax.nn.softmax(hg_lower_bounds.astype(F32), axis=1)
    lbs = jnp.cumsum(lbs, axis=1) - lbs[:, :1]

    n_cond = nsmp + 1
    pad = (-n_cond) % 8
    cond = jnp.concatenate([c, c_ctx[None], jnp.zeros((pad, D), F32)], axis=0)

    mods = [modulation(cond, mod_w, mod_b, layer) for layer in range(depth)]
    w1_all, w2_all = mlp_w1.astype(BF16), mlp_w2.astype(BF16)
    hg_in_all, hg_out_all = hg_w_in.astype(BF16), hg_w_out.astype(BF16)

    xp, xs = x_prompt, x_sample
    hp = hs = None
    l_ckv, l_kpe, l_s5r, l_s5i, l_hg = [], [], [], [], []
    for layer in range(depth):
        j = layer // 2
        m = mods[layer]
        ms = [m[:nsmp, i * D:(i + 1) * D][:, None, :] for i in range(6)]
        mp = [m[nsmp:nsmp + 1, i * D:(i + 1) * D][:, None, :] for i in range(6)]
        if layer == 0:
            hp = norm_mod(xp, norm_g[layer, 0], mp[1], mp[0])
            hs = norm_mod(xs, norm_g[layer, 0], ms[1], ms[0])
        if layer % 2 == 0:
            wts = _ab_weights(ab_w_in[j], ab_w_out[j], mla_q_norm[j], mla_kv_norm[j], mla_w_uq[j], mla_w_ukv[j],
                              s5w, q_rank, kv_rank, mla_heads)
            s5m = _s5_weights(s5_log_dt[j], s5_lam_re[j], s5_lam_im[j], s5_b_re[j], s5_b_im[j],
                              s5_c_re[j], s5_c_im[j], s5_d[j])
            nb = s5m[0].shape[0]
            zero_h0 = jnp.zeros((nb, xp.shape[0], 1, s5m[3].shape[-1]), F32)
            yp, ckv, kpe, fin = _ab_mixer(hp, wts, s5m, s5_glu_w[j], s5_glu_b[j], heads=mla_heads,
                                          kv_rank=kv_rank, ctx_ckv=None, ctx_kpe=None, h0=zero_h0, rope=False)
            h0s = _pack_s5_state(state_s5_re[:, j], state_s5_im[:, j], nb)
            ys, _, _, _ = _ab_mixer(hs, wts, s5m, s5_glu_w[j], s5_glu_b[j], heads=mla_heads, kv_rank=kv_rank,
                                    ctx_ckv=cache_ckv[:, j], ctx_kpe=cache_kpe[:, j], h0=h0s, rope=True)
            hr, hi = _unpack_s5_state(fin, groups, states)
            l_ckv.append(ckv.reshape(xp.shape[0], xp.shape[1], kv_rank))
            l_kpe.append(kpe.reshape(xp.shape[0], xp.shape[1], 64))
            l_s5r.append(hr)
            l_s5i.append(hi)
        else:
            yp, st = _hg_mixer(hp, hg_in_all, hg_out_all, j, lbs[:, layer], hg_out_norm[j], None, hg_heads)
            ys, _ = _hg_mixer(hs, hg_in_all, hg_out_all, j, lbs[:, layer], hg_out_norm[j], state_hgrn[:, j],
                              hg_heads)
            l_hg.append(st)
        xp, hp = resid_norm(xp, yp, norm_g[layer, 1], mp[2], (norm_g[layer, 2], mp[4], mp[3]))
        xs, hs = resid_norm(xs, ys, norm_g[layer, 1], ms[2], (norm_g[layer, 2], ms[4], ms[3]))
        yp = _mlp(hp, w1_all, w2_all, layer)
        ys = _mlp(hs, w1_all, w2_all, layer)
        if layer + 1 < depth:
            m_n = mods[layer + 1]
            nxt_s = (norm_g[layer + 1, 0], m_n[:nsmp, D:2 * D][:, None, :], m_n[:nsmp, 0:D][:, None, :])
            nxt_p = (norm_g[layer + 1, 0], m_n[nsmp:nsmp + 1, D:2 * D][:, None, :],
                     m_n[nsmp:nsmp + 1, 0:D][:, None, :])
        else:
            nxt_s = nxt_p = None
        xp, hp = resid_norm(xp, yp, norm_g[layer, 3], mp[5], nxt_p)
        xs, hs = resid_norm(xs, ys, norm_g[layer, 3], ms[5], nxt_s)
    new_ckv = jnp.stack(l_ckv, axis=1)
    new_kpe = jnp.stack(l_kpe, axis=1)
    new_s5_re = jnp.stack(l_s5r, axis=1)
    new_s5_im = jnp.stack(l_s5i, axis=1)
    new_hgrn = jnp.stack(l_hg, axis=1)
    return (xp, xs, new_ckv, new_kpe, new_s5_re, new_s5_im, new_hgrn)
```

```python
import functools
import math

import jax
import jax.numpy as jnp
from jax import lax
from jax.experimental import pallas as pl
from jax.experimental.pallas import tpu as pltpu

F32 = jnp.float32
BF16 = jnp.bfloat16

EPS = 1e-6
GRID_W = 64
ROPE_BASE = 10000.0
LANES = 128
S5_CHUNK = 16
HG_SUB = 32
HG_BLOCK = 2 * HG_SUB
VMEM_LIMIT = 56 * 1024 * 1024
MM_VMEM_BUDGET = 42 * 1024 * 1024
ATT_HEADS = 2
HG_POST_HEADS = 8
HG_HEADS = 4
HG_BLOCKS = 16
S5_SEQS = 4


def _pick(n, prefs):
    for p in prefs:
        if n % p == 0:
            return p
    return n


def _cparams(sem):
    return pltpu.CompilerParams(dimension_semantics=sem, vmem_limit_bytes=VMEM_LIMIT)


def _sigmoid(x):
    return 0.5 + 0.5 * jnp.tanh(0.5 * x)


def _silu(x):
    return x * _sigmoid(x)


def _gelu_tanh(x):
    c = math.sqrt(2.0 / math.pi)
    return 0.5 * x * (1.0 + jnp.tanh(c * (x + 0.044715 * (x * x * x))))


def _rms(x):
    return x * lax.rsqrt(jnp.mean(x * x, axis=-1, keepdims=True) + EPS)


def _mm_kernel(*refs, nk, n_aux, epilogue, a_blocked, out_mode, fold):
    a_ref, w_ref = refs[0], refs[1]
    aux_refs = refs[2:2 + n_aux]
    o_ref = refs[2 + n_aux]
    scratch = list(refs[3 + n_aux:])

    def load_a():
        if a_blocked:
            return jnp.concatenate([a_ref[c] for c in range(a_ref.shape[0])], axis=-1).astype(BF16)
        return a_ref[0].astype(BF16)

    def finish(acc):
        aux = []
        for r in aux_refs:
            v = r[...]
            aux.append(v)
        out = epilogue(acc, *aux) if epilogue is not None else acc
        if out_mode == "fold_rows":
            slab_ref = scratch.pop()
            rows = slab_ref.shape[1] // fold
            for c in range(o_ref.shape[0]):
                slab_ref[c] = out[:, c * LANES:(c + 1) * LANES]
            for c in range(o_ref.shape[0]):
                for t in range(fold):
                    o_ref[c, :, t * LANES:(t + 1) * LANES] = (
                        slab_ref[c, pl.ds(t, rows, stride=fold), :].astype(o_ref.dtype))
        elif out_mode == "unfold_rows":
            rows = out.shape[0]
            for t in range(fold):
                o_ref[0, pl.ds(t, rows, stride=fold), :] = out[:, t * LANES:(t + 1) * LANES].astype(o_ref.dtype)
        else:
            o_ref[0] = out.astype(o_ref.dtype)

    if nk == 1:
        finish(jnp.dot(load_a(), w_ref[0].astype(BF16), preferred_element_type=F32))
        return

    acc_ref = scratch.pop(0)
    k = pl.program_id(3)

    @pl.when(k == 0)
    def _():
        acc_ref[...] = jnp.zeros_like(acc_ref)

    acc_ref[...] += jnp.dot(load_a(), w_ref[0].astype(BF16), preferred_element_type=F32)

    @pl.when(k == nk - 1)
    def _():
        finish(acc_ref[...])


def _pick_tk(K, tm, tn, a_bytes, w_bytes, out_bytes, aux_bytes):
    for tk in (K, 4096, 2048, 1024, 512, 256, 128):
        if tk > K or K % tk:
            continue
        acc = 0 if tk == K else 4 * tm * tn
        est = 2 * (tm * tk * a_bytes + tk * tn * w_bytes) + 2 * tm * tn * out_bytes + acc + 2 * aux_bytes
        if est <= MM_VMEM_BUDGET:
            return tk
    return LANES


def mm(a, w, *, wg=0, epilogue=None, aux=(), out_dtype=F32, tm=None, tn=None, tk=None,
       a_blocked=False, out_mode="plain", fold=1, name="mm"):
    _, K, N = w.shape
    G = 1 if a_blocked else a.shape[0]
    M = a.shape[1]
    tm = tm or _pick(M, (1024, 512, 256, 128))
    tn = tn or _pick(N, (1024, 512, 256, 128))
    if tk is None:
        aux_bytes = sum(math.prod(bshape) * arr.dtype.itemsize for arr, bshape, _ in aux)
        tk = _pick_tk(K, tm, tn, a.dtype.itemsize, w.dtype.itemsize, jnp.dtype(out_dtype).itemsize, aux_bytes)
    assert M % tm == 0 and N % tn == 0 and K % tk == 0, (M, N, K, tm, tn, tk)
    nk = K // tk
    grid = (G, M // tm, N // tn, nk)
    if a_blocked:
        a_spec = pl.BlockSpec((tk // LANES, tm, LANES), lambda g, i, j, k: (k, i, 0))
    else:
        a_spec = pl.BlockSpec((1, tm, tk), lambda g, i, j, k: (g, i, k))
    w_spec = pl.BlockSpec((1, tk, tn), lambda g, i, j, k: (wg + g, k, j))
    aux_arrays, aux_specs = [], []
    for arr, bshape, imap in aux:
        aux_arrays.append(arr)
        aux_specs.append(pl.BlockSpec(bshape, functools.partial(lambda g, i, j, k, f: f(g, i, j), f=imap)))
    scratch = [] if nk == 1 else [pltpu.VMEM((tm, tn), F32)]
    if out_mode == "fold_rows":
        assert G == 1 and tm % fold == 0
        out_shape = jax.ShapeDtypeStruct((N // LANES, M // fold, fold * LANES), out_dtype)
        out_spec = pl.BlockSpec((tn // LANES, tm // fold, fold * LANES), lambda g, i, j, k: (j, i, 0))
        scratch.append(pltpu.VMEM((tn // LANES, tm, LANES), F32))
    elif out_mode == "unfold_rows":
        assert tn == N == fold * LANES
        out_shape = jax.ShapeDtypeStruct((G, M * fold, LANES), out_dtype)
        out_spec = pl.BlockSpec((1, tm * fold, LANES), lambda g, i, j, k: (g, i, 0))
    else:
        out_shape = jax.ShapeDtypeStruct((G, M, N), out_dtype)
        out_spec = pl.BlockSpec((1, tm, tn), lambda g, i, j, k: (g, i, j))
    kern = functools.partial(_mm_kernel, nk=nk, n_aux=len(aux_arrays), epilogue=epilogue,
                             a_blocked=a_blocked, out_mode=out_mode, fold=fold)
    return pl.pallas_call(
        kern, grid=grid, in_specs=[a_spec, w_spec] + aux_specs, out_specs=out_spec,
        out_shape=out_shape, scratch_shapes=scratch, name=name,
        compiler_params=_cparams(("parallel", "parallel", "parallel", "arbitrary")),
    )(a, w, *aux_arrays)


def mm2d(a, w, **kw):
    out = mm(a[None], w if w.ndim == 3 else w[None], **kw)
    return out if kw.get("out_mode", "plain") != "plain" else out[0]


def _epi_relu2(acc):
    r = jnp.maximum(acc, 0.0)
    return r * r


def _epi_rms(acc, g):
    return _rms(acc) * g


def _epi_ckv(acc, g, *, rank):
    return jnp.concatenate([_rms(acc[:, :rank]) * g, acc[:, rank:]], axis=-1)


def _epi_table(acc, t):
    reps = acc.shape[1] // t.shape[1]
    return acc * jnp.concatenate([t] * reps, axis=-1)


def _epi_kadd(acc, kpks, t):
    kr2 = kpks[:, :LANES] * t[:, :LANES] + kpks[:, LANES:] * t[:, LANES:]
    blk = jnp.concatenate([jnp.zeros_like(kr2), kr2], axis=-1)
    reps = acc.shape[1] // blk.shape[1]
    return acc + jnp.concatenate([blk] * reps, axis=-1)


def _epi_add_gelu(acc, y0):
    return _gelu_tanh(acc + y0[0])


def _epi_glu(acc, gy, b):
    y = jnp.concatenate([gy[c] for c in range(gy.shape[0])], axis=-1).astype(F32)
    return y * _sigmoid(acc + b)


def _mod_kernel(c_ref, w_ref, b_ref, o_ref, *, nk):
    k = pl.program_id(1)

    @pl.when(k == 0)
    def _():
        o_ref[...] = jnp.zeros_like(o_ref)

    a = _silu(c_ref[...]).astype(BF16)
    o_ref[...] += jnp.dot(a, w_ref[...].astype(BF16), preferred_element_type=F32)

    @pl.when(k == nk - 1)
    def _():
        o_ref[...] += b_ref[...]


def modulation(cond, w, b, layer):
    rows, d = cond.shape
    n = w.shape[2]
    tn = _pick(n, (2048, 1024, 512, 256, 128))
    tk = _pick(d, (1024, 512, 256, 128))
    nk = d // tk
    return pl.pallas_call(
        functools.partial(_mod_kernel, nk=nk), grid=(n // tn, nk),
        in_specs=[pl.BlockSpec((rows, tk), lambda j, k: (0, k)),
                  pl.BlockSpec((None, tk, tn), lambda j, k: (layer, k, j)),
                  pl.BlockSpec((None, 1, tn), lambda j, k: (layer, 0, j))],
        out_specs=pl.BlockSpec((rows, tn), lambda j, k: (0, j)),
        out_shape=jax.ShapeDtypeStruct((rows, n), F32), name="modulation",
        compiler_params=_cparams(("parallel", "arbitrary")),
    )(cond, w, b[:, None, :])


def _norm_mod_kernel(x_ref, g_ref, sc_ref, sh_ref, o_ref):
    y = _rms(x_ref[0]) * g_ref[...]
    o_ref[0] = (y * (1.0 + sc_ref[0]) + sh_ref[0]).astype(o_ref.dtype)


def norm_mod(x, g, sc, sh):
    nseq, L, D = x.shape
    tr = _pick(L, (256, 128))
    per_seq = sc.shape[0] == nseq and nseq > 1
    smap = (lambda b, i: (b, 0, 0)) if per_seq else (lambda b, i: (0, 0, 0))
    return pl.pallas_call(
        _norm_mod_kernel, grid=(nseq, L // tr),
        in_specs=[pl.BlockSpec((1, tr, D), lambda b, i: (b, i, 0)),
                  pl.BlockSpec((1, D), lambda b, i: (0, 0)),
                  pl.BlockSpec((1, 1, D), smap), pl.BlockSpec((1, 1, D), smap)],
        out_specs=pl.BlockSpec((1, tr, D), lambda b, i: (b, i, 0)),
        out_shape=jax.ShapeDtypeStruct((nseq, L, D), BF16), name="norm_mod",
        compiler_params=_cparams(("parallel", "parallel")),
    )(x, g[None], sc, sh)


def _resid_kernel(x_ref, y_ref, g1_ref, gt_ref, *rest, with_h):
    xn = x_ref[0] + gt_ref[0] * (_rms(y_ref[0].astype(F32)) * g1_ref[...])
    if with_h:
        g2_ref, sc_ref, sh_ref, xo_ref, h_ref = rest
        xo_ref[0] = xn
        h_ref[0] = ((_rms(xn) * g2_ref[...]) * (1.0 + sc_ref[0]) + sh_ref[0]).astype(h_ref.dtype)
    else:
        (xo_ref,) = rest
        xo_ref[0] = xn


def resid_norm(x, y, g1, gate, nxt=None):
    nseq, L, D = x.shape
    tr = _pick(L, (256, 128))
    per_seq = gate.shape[0] == nseq and nseq > 1
    smap = (lambda b, i: (b, 0, 0)) if per_seq else (lambda b, i: (0, 0, 0))
    row = pl.BlockSpec((1, tr, D), lambda b, i: (b, i, 0))
    vec = pl.BlockSpec((1, D), lambda b, i: (0, 0))
    mod = pl.BlockSpec((1, 1, D), smap)
    args = [x, y, g1[None], gate]
    specs = [row, row, vec, mod]
    out_shape = [jax.ShapeDtypeStruct((nseq, L, D), F32)]
    out_specs = [row]
    if nxt is not None:
        g2, sc, sh = nxt
        args += [g2[None], sc, sh]
        specs += [vec, mod, mod]
        out_shape.append(jax.ShapeDtypeStruct((nseq, L, D), BF16))
        out_specs.append(row)
    res = pl.pallas_call(
        functools.partial(_resid_kernel, with_h=nxt is not None), grid=(nseq, L // tr),
        in_specs=specs, out_specs=out_specs, out_shape=out_shape, name="resid_norm",
        compiler_params=_cparams(("parallel", "parallel")),
    )(*args)
    return (res[0], res[1]) if nxt is not None else (res[0], None)


def _attn_kernel(q_ref, k_ref, v_ref, o_ref, *, hp):
    scores = []
    for h in range(hp):
        qk = slice(h * 2 * LANES, (h + 1) * 2 * LANES)
        scores.append(lax.dot_general(q_ref[0, :, qk], k_ref[0, :, qk], (((1,), (1,)), ((), ())),
                                      preferred_element_type=F32))
    probs = []
    for s in scores:
        m = jnp.max(s, axis=-1, keepdims=True)
        p = jnp.exp2(s - m)
        probs.append((p.astype(BF16), jnp.sum(p, axis=-1, keepdims=True)))
    for h, (p, l) in enumerate(probs):
        vo = slice(h * LANES, (h + 1) * LANES)
        o = jnp.dot(p, v_ref[0, :, vo], preferred_element_type=F32)
        o_ref[0, :, vo] = (o / l).astype(o_ref.dtype)


def attention(q, kcat, v, heads):
    nseq, L, _ = q.shape
    Lk = kcat.shape[1]
    tq = _pick(L, (256, 128))
    hp = ATT_HEADS if heads % ATT_HEADS == 0 else 1
    return pl.pallas_call(
        functools.partial(_attn_kernel, hp=hp), grid=(nseq, heads // hp, L // tq),
        in_specs=[pl.BlockSpec((1, tq, hp * 2 * LANES), lambda b, h, i: (b, i, h)),
                  pl.BlockSpec((1, Lk, hp * 2 * LANES), lambda b, h, i: (b, 0, h)),
                  pl.BlockSpec((1, Lk, hp * LANES), lambda b, h, i: (b, 0, h))],
        out_specs=pl.BlockSpec((1, tq, hp * LANES), lambda b, h, i: (b, i, h)),
        out_shape=jax.ShapeDtypeStruct((nseq, L, heads * LANES), BF16), name="mla_attention",
        compiler_params=_cparams(("parallel", "parallel", "arbitrary")),
    )(q, kcat, v)


def _s5_scan_kernel(g_ref, ad_ref, h0_ref, sin_ref, fin_ref, *, n, sb, half):
    ad = ad_ref[0]
    afr, afi = ad[:, 0:half], ad[:, half:2 * half]
    abr, abi = ad[:, 2 * half:3 * half], ad[:, 3 * half:4 * half]

    def body(k, carry):
        out = []
        for s in range(sb):
            fr, fi, br, bi = carry[4 * s:4 * s + 4]
            row_f = pl.ds(s * n + k, 1)
            row_b = pl.ds(s * n + n - 1 - k, 1)
            sin_ref[0, row_f, 0:2 * half] = jnp.concatenate([fr, fi], axis=-1)
            sin_ref[0, row_b, 2 * half:4 * half] = jnp.concatenate([br, bi], axis=-1)
            gf = g_ref[0, row_f, 0:2 * half]
            gb = g_ref[0, row_b, 2 * half:4 * half]
            out += [afr * fr - afi * fi + gf[:, :half], afr * fi + afi * fr + gf[:, half:],
                    abr * br - abi * bi + gb[:, :half], abr * bi + abi * br + gb[:, half:]]
        return tuple(out)

    init = []
    for s in range(sb):
        h0 = h0_ref[0, s]
        init += [h0[:, 0:half], h0[:, half:2 * half], h0[:, 2 * half:3 * half], h0[:, 3 * half:4 * half]]
    fin = lax.fori_loop(0, n, body, tuple(init))
    for s in range(sb):
        fin_ref[0, s] = jnp.concatenate(fin[4 * s:4 * s + 4], axis=-1)


def s5_scan(g, ad, h0, n):
    nb, rows, W = g.shape
    nseq = rows // n
    sb = S5_SEQS if nseq % S5_SEQS == 0 else 1
    return pl.pallas_call(
        functools.partial(_s5_scan_kernel, n=n, sb=sb, half=W // 4), grid=(nb, nseq // sb),
        in_specs=[pl.BlockSpec((1, sb * n, W), lambda j, b: (j, b, 0)),
                  pl.BlockSpec((1, 1, W), lambda j, b: (j, 0, 0)),
                  pl.BlockSpec((1, sb, 1, W), lambda j, b: (j, b, 0, 0))],
        out_specs=[pl.BlockSpec((1, sb * n, W), lambda j, b: (j, b, 0)),
                   pl.BlockSpec((1, sb, 1, W), lambda j, b: (j, b, 0, 0))],
        out_shape=[jax.ShapeDtypeStruct((nb, rows, W), F32),
                   jax.ShapeDtypeStruct((nb, nseq, 1, W), F32)], name="s5_scan",
        compiler_params=_cparams(("parallel", "parallel")),
    )(g, ad, h0)


def _hg_masks(rev):
    T = HG_BLOCK
    row = lax.broadcasted_iota(jnp.int32, (T, LANES), 0)
    first = row < HG_SUB
    r2 = lax.broadcasted_iota(jnp.int32, (T, 2 * T), 0)
    c2 = lax.broadcasted_iota(jnp.int32, (T, 2 * T), 1)
    s2 = c2 & (T - 1)
    sub_bits = HG_SUB.bit_length() - 1
    other_sub = (r2 ^ s2) >> sub_bits
    causal = (s2 >= r2) if rev else (s2 <= r2)
    keep = jnp.where(other_sub == (c2 >> (sub_bits + 1)), jnp.where(causal, 1, 0), 0) > 0
    rin = row & (HG_SUB - 1)
    steps = (1, 2, 4, 8, 16)
    scan = [(rin < HG_SUB - s) if rev else (rin >= s) for s in steps]
    return steps, scan, first, keep


def _hg_direction(q, z, v, lb, st, rev, masks):
    T = HG_BLOCK
    steps, scan, first, keep = masks
    qa = _silu(q) * (LANES ** -0.5)
    th = 0.5 * jnp.tanh(0.5 * z)
    sig = 0.5 + th
    nsig = 0.5 - th
    kk = (1.0 - lb) * nsig
    g = jnp.log(lb + (1.0 - lb) * sig)

    b = g
    for s, ok in zip(steps, scan):
        b = b + jnp.where(ok, pltpu.roll(b, (T - s) if rev else s, axis=0), 0.0)
    if not rev:
        b0, b1 = b[HG_SUB - 1:HG_SUB], b[T - 1:T]
    else:
        b0, b1 = b[0:1], b[HG_SUB:HG_SUB + 1]
    bsub = jnp.where(first, b0, b1)
    qh = qa * jnp.exp(b)
    kd = kk * jnp.exp(-b)
    ke = kk * jnp.exp(bsub - b)

    kcat = jnp.concatenate([kd, ke], axis=0).astype(BF16)
    att = lax.dot_general(qh.astype(BF16), kcat, (((1,), (1,)), ((), ())), preferred_element_type=F32)
    att = jnp.where(keep, att, 0.0)
    att = att + pltpu.roll(att, T, axis=1)
    vb = v.astype(BF16)
    o = jnp.dot(att[:, :T].astype(BF16), vb, preferred_element_type=F32)

    if not rev:
        dq = jnp.where(first, 1.0, jnp.exp(b0))
        ek = jnp.where(first, jnp.exp(b1), 1.0)
    else:
        dq = jnp.where(first, jnp.exp(b1), 1.0)
        ek = jnp.where(first, 1.0, jnp.exp(b0))
    o = o + lax.dot_general((qh * dq).astype(BF16), st.astype(BF16), (((1,), (1,)), ((), ())),
                            preferred_element_type=F32)
    upd = lax.dot_general(vb, (ke * ek).astype(BF16), (((0,), (0,)), ((), ())), preferred_element_type=F32)
    st_new = st * jnp.exp(b0 + b1) + upd
    return o, st_new


def _hg_kernel(*refs, nsteps, nb, hp, has_init):
    if has_init:
        (qf_ref, zf_ref, vf_ref, qb_ref, zb_ref, vb_ref, lb_ref, s0_ref,
         of_ref, ob_ref, so_ref, stf_ref, stb_ref) = refs
    else:
        (qf_ref, zf_ref, vf_ref, qb_ref, zb_ref, vb_ref, lb_ref,
         of_ref, ob_ref, so_ref, stf_ref, stb_ref) = refs
    i = pl.program_id(2)
    T = HG_BLOCK

    @pl.when(i == 0)
    def _():
        for h in range(hp):
            if has_init:
                stf_ref[h] = s0_ref[0, 0, h].T
                stb_ref[h] = s0_ref[0, 1, h].T
            else:
                stf_ref[h] = jnp.zeros((LANES, LANES), F32)
                stb_ref[h] = jnp.zeros((LANES, LANES), F32)

    lb = lb_ref[...]
    masks_f = _hg_masks(False)
    masks_b = _hg_masks(True)
    for h in range(hp):
        lanes = slice(h * LANES, (h + 1) * LANES)
        st_f = stf_ref[h]
        st_b = stb_ref[h]
        for blk in range(nb):
            rows = slice(blk * T, (blk + 1) * T)
            o_f, st_f = _hg_direction(qf_ref[0, rows, lanes], zf_ref[0, rows, lanes], vf_ref[0, rows, lanes],
                                      lb[0:1, lanes], st_f, False, masks_f)
            of_ref[0, rows, lanes] = o_f
        for blk in reversed(range(nb)):
            rows = slice(blk * T, (blk + 1) * T)
            o_b, st_b = _hg_direction(qb_ref[0, rows, lanes], zb_ref[0, rows, lanes], vb_ref[0, rows, lanes],
                                      lb[1:2, lanes], st_b, True, masks_b)
            ob_ref[0, rows, lanes] = o_b
        stf_ref[h] = st_f
        stb_ref[h] = st_b

        @pl.when(i == nsteps - 1)
        def _(h=h, st_f=st_f, st_b=st_b):
            so_ref[0, 0, h] = st_f.T
            so_ref[0, 1, h] = st_b.T


def hgrn2(z, lb, s0, heads):
    nseq, L, _ = z.shape
    hp = HG_HEADS if heads % HG_HEADS == 0 else 1
    nb = max(n for n in range(1, HG_BLOCKS + 1) if L % (n * HG_BLOCK) == 0)
    T = nb * HG_BLOCK
    nsteps = L // T
    HB = heads // hp

    def col(off, rev):
        if rev:
            return lambda b, h, i: (b, nsteps - 1 - i, off + h)
        return lambda b, h, i: (b, i, off + h)

    tile = lambda off, rev: pl.BlockSpec((1, T, hp * LANES), col(off, rev))
    in_specs = [tile(0, False), tile(HB, False), tile(3 * HB, False),
                tile(0, True), tile(2 * HB, True), tile(3 * HB, True),
                pl.BlockSpec((2, hp * LANES), lambda b, h, i: (0, h))]
    args = [z, z, z, z, z, z, lb]
    st_spec = pl.BlockSpec((1, 2, hp, LANES, LANES), lambda b, h, i: (b, 0, h, 0, 0))
    if s0 is not None:
        in_specs.append(st_spec)
        args.append(s0)
    o_shape = jax.ShapeDtypeStruct((nseq, L, heads * LANES), F32)
    st_scratch = pltpu.VMEM((hp, LANES, LANES), F32)
    return pl.pallas_call(
        functools.partial(_hg_kernel, nsteps=nsteps, nb=nb, hp=hp, has_init=s0 is not None),
        grid=(nseq, HB, nsteps), in_specs=in_specs,
        out_specs=[tile(0, False), tile(0, True), st_spec],
        out_shape=[o_shape, o_shape, jax.ShapeDtypeStruct((nseq, 2, heads, LANES, LANES), F32)],
        scratch_shapes=[st_scratch, st_scratch], name="hgrn2",
        compiler_params=_cparams(("parallel", "parallel", "arbitrary")),
    )(*args)


def _hg_post_kernel(of_ref, ob_ref, g_ref, gn_ref, o_ref, *, hp):
    for h in range(hp):
        lanes = slice(h * LANES, (h + 1) * LANES)
        o = of_ref[0, :, lanes] + ob_ref[0, :, lanes]
        o_ref[0, :, lanes] = ((_rms(o) * gn_ref[:, lanes]) * _silu(g_ref[0, :, lanes])).astype(o_ref.dtype)


def hg_post(o_f, o_b, z, out_norm, heads):
    nseq, L, _ = o_f.shape
    tr = _pick(L, (256, 128))
    hp = HG_POST_HEADS if heads % HG_POST_HEADS == 0 else 1
    tile = pl.BlockSpec((1, tr, hp * LANES), lambda b, i, h: (b, i, h))
    return pl.pallas_call(
        functools.partial(_hg_post_kernel, hp=hp), grid=(nseq, L // tr, heads // hp),
        in_specs=[tile, tile, pl.BlockSpec((1, tr, hp * LANES), lambda b, i, h: (b, i, 4 * (heads // hp) + h)),
                  pl.BlockSpec((1, hp * LANES), lambda b, i, h: (0, h))],
        out_specs=tile, out_shape=jax.ShapeDtypeStruct(o_f.shape, BF16), name="hg_post",
        compiler_params=_cparams(("parallel", "parallel", "parallel")),
    )(o_f, o_b, z, out_norm[None])


def _rope_tables(n_l):
    rope = 64
    half = rope // 2
    rows = n_l // GRID_W
    row = jnp.repeat(jnp.arange(rows), GRID_W).astype(F32)
    col = jnp.tile(jnp.arange(GRID_W), rows).astype(F32)
    inv = ROPE_BASE ** (-jnp.arange(0, half, 2, dtype=F32) / half)
    ar = row[:, None] * inv[None]
    ac = col[:, None] * inv[None]
    cos = jnp.concatenate([jnp.cos(ar), jnp.cos(ar), jnp.cos(ac), jnp.cos(ac)], axis=-1)
    sin = jnp.concatenate([jnp.sin(ar), jnp.sin(ar), jnp.sin(ac), jnp.sin(ac)], axis=-1)
    return cos, sin


def _rot_cols(w):
    return jnp.concatenate([-w[..., 16:32], w[..., 0:16], -w[..., 48:64], w[..., 32:48]], axis=-1)


def _s5_weights(log_dt, lam_re, lam_im, b_re, b_im, c_re, c_im, d_skip):
    D = S5_CHUNK
    G, P = lam_re.shape[1], lam_re.shape[2]
    C = b_re.shape[-1]
    gl = LANES // C
    nb = G // gl
    dt = jnp.exp(log_dt.astype(F32))[..., None]
    lr, li = lam_re.astype(F32), lam_im.astype(F32)
    mag = jnp.exp(lr * dt)
    ar, ai = mag * jnp.cos(li * dt), mag * jnp.sin(li * dt)
    den = lr * lr + li * li
    cr_ = ((ar - 1.0) * lr + ai * li) / den
    ci_ = (ai * lr - (ar - 1.0) * li) / den
    br, bi = b_re.astype(F32), b_im.astype(F32)
    bbr = cr_[..., None] * br - ci_[..., None] * bi
    bbi = cr_[..., None] * bi + ci_[..., None] * br
    cr, ci = c_re.astype(F32), c_im.astype(F32)
    def powers(n):
        n = n.astype(F32)[:, None, None, None]
        pmag = jnp.exp(n * (lr * dt)[None])
        return pmag * jnp.cos(n * (li * dt)[None]), pmag * jnp.sin(n * (li * dt)[None])

    def times_b(p_r, p_i):
        return (p_r[..., None] * bbr[None] - p_i[..., None] * bbi[None],
                p_r[..., None] * bbi[None] + p_i[..., None] * bbr[None])

    steps = jnp.arange(D)
    pr, pi = powers(jnp.arange(D + 1))
    abr, abi = times_b(pr[:D], pi[:D])
    abr_dn, abi_dn = times_b(*powers(D - 1 - steps))
    pr_dn, pi_dn = powers(D - steps)
    half = gl * P
    lane_group = jnp.arange(LANES) // C
    tok_lane_group = jnp.tile(lane_group, D)
    state_group = jnp.arange(half) // P

    def response(x, a_r, a_i):
        return jnp.einsum('gcp,ngpk->ngkc', cr[x], a_r) - jnp.einsum('gcp,ngpk->ngkc', ci[x], a_i)

    kf = response(0, abr[:, 0], abi[:, 0])
    kb = response(1, abr_dn[:, 1], abi_dn[:, 1])
    skip = d_skip.astype(F32)[:, :, None] * jnp.eye(C, dtype=F32)[None]
    zpad = jnp.zeros((D - 1,) + kf.shape[1:], F32)
    lagk = (jnp.concatenate([zpad, kf], axis=0) + jnp.concatenate([kb, zpad], axis=0)
            + jnp.concatenate([zpad, skip[None], zpad], axis=0))
    bd = jnp.tile(lagk.reshape(2 * D - 1, nb, LANES, C), (1, 1, 1, gl))
    bd = jnp.where(lane_group[:, None] == lane_group[None, :], bd, 0.0).astype(BF16)
    lag_idx = jnp.arange(D)[None, :] - jnp.arange(D)[:, None] + (D - 1)
    w_t = bd[lag_idx].transpose(2, 0, 3, 1, 4).reshape(nb, D * LANES, D * LANES)

    def to_state(x):
        rows = x.reshape(D, nb, gl, P, C).transpose(1, 0, 2, 4, 3).reshape(nb, D * LANES, P)
        tiled = jnp.tile(rows, (1, 1, gl))
        return jnp.where(tok_lane_group[:, None] == state_group[None, :], tiled, 0.0).astype(BF16)

    w_b = jnp.concatenate([to_state(abr_dn[:, 0]), to_state(abi_dn[:, 0]),
                           to_state(abr[:, 1]), to_state(abi[:, 1])], axis=-1)

    def from_state(y):
        cols = y.reshape(D, nb, gl, C, P).transpose(1, 4, 0, 2, 3).reshape(nb, P, D * LANES)
        tiled = jnp.tile(cols, (1, gl, 1))
        return jnp.where(state_group[:, None] == tok_lane_group[None, :], tiled, 0.0).astype(BF16)

    def c_times(x, p_r, p_i):
        re = cr[x][None] * p_r[:, :, None, :] - ci[x][None] * p_i[:, :, None, :]
        im = cr[x][None] * p_i[:, :, None, :] + ci[x][None] * p_r[:, :, None, :]
        return re, -im

    cf = c_times(0, pr[1:D + 1, 0], pi[1:D + 1, 0])
    cb = c_times(1, pr_dn[:, 1], pi_dn[:, 1])
    w_c = jnp.concatenate([from_state(cf[0]), from_state(cf[1]), from_state(cb[0]), from_state(cb[1])], axis=1)
    adr = pr[D].reshape(2, nb, half)
    adi = pi[D].reshape(2, nb, half)
    ad = jnp.concatenate([adr[0], adi[0], adr[1], adi[1]], axis=-1)[:, None, :]
    return w_t, w_b, w_c, ad


def _ab_weights(w_in, w_out, q_norm, kv_norm, w_uq, w_ukv, s5w, q_rank, kv_rank, heads):
    rope, nope, vdim = 64, 128, 128
    o1, o2, o3 = s5w, s5w + q_rank, s5w + q_rank + kv_rank
    w_u = w_in[:, :o1].astype(BF16)
    w_q = w_in[:, o1:o2].astype(BF16)
    w_kp = w_in[:, o3:]
    w_ks = _rot_cols(w_kp)
    w_small = jnp.concatenate([w_in[:, o2:o3], w_kp, w_kp, w_ks, w_ks], axis=-1).astype(BF16)
    uq = w_uq.reshape(q_rank, heads, nope + rope)
    uq_aug = jnp.concatenate([uq[..., :nope], uq[..., nope:], _rot_cols(uq[..., nope:])], axis=-1)
    uq_aug = uq_aug.reshape(q_rank, heads * 2 * LANES).astype(BF16)
    ukv = w_ukv.reshape(kv_rank, heads, nope + vdim)
    ukn_aug = jnp.concatenate([ukv[..., :nope], jnp.zeros_like(ukv[..., :nope])], axis=-1)
    ukn_aug = ukn_aug.reshape(kv_rank, heads * 2 * LANES).astype(BF16)
    uv = ukv[..., nope:].reshape(kv_rank, heads * vdim).astype(BF16)
    return dict(w_u=w_u, w_q=w_q, w_small=w_small, uq_aug=uq_aug, ukn_aug=ukn_aug, uv=uv,
                w_out=w_out.astype(BF16), q_norm=q_norm[None], kv_norm=kv_norm[None])


def _ab_mixer(h, wts, s5m, glu_w, glu_b, *, heads, kv_rank, ctx_ckv, ctx_kpe, h0, rope):
    nseq, L, D = h.shape
    M = nseq * L
    h2 = h.reshape(M, D)
    w_t, w_b, w_c, ad = s5m
    nb = w_t.shape[0]
    scale = (128 + 64) ** -0.5 * math.log2(math.e)

    u2 = mm2d(h2, wts['w_u'], out_dtype=BF16, out_mode="fold_rows", fold=S5_CHUNK, name="ab_in_u")
    tmq = _pick(L, (1024, 512, 256, 128))
    qlat = mm2d(h2, wts['w_q'], epilogue=_epi_rms, tn=wts['w_q'].shape[1], tm=_pick(M, (512, 256, 128)),
                aux=[(wts['q_norm'], (1, wts['w_q'].shape[1]), lambda g, i, j: (0, 0))],
                out_dtype=BF16, name="ab_in_q")
    nsm = wts['w_small'].shape[1]
    small = mm2d(h2, wts['w_small'], epilogue=functools.partial(_epi_ckv, rank=kv_rank), tn=nsm,
                 tm=_pick(M, (512, 256, 128)),
                 aux=[(wts['kv_norm'], (1, kv_rank), lambda g, i, j: (0, 0))], name="ab_in_kv")
    ckv = small[:, :kv_rank]
    kpks = small[:, kv_rank:]

    ones = jnp.ones((tmq, 64), F32)
    zeros = jnp.zeros((tmq, 64), F32)
    if rope:
        cos, sin = _rope_tables(L)
    else:
        cos, sin = ones, zeros
    one128 = jnp.ones((cos.shape[0], LANES), F32)
    tq = jnp.concatenate([one128, cos, sin], axis=-1) * scale
    nq_t = tq.shape[0] // tmq
    q = mm2d(qlat, wts['uq_aug'], epilogue=_epi_table, tm=tmq,
             aux=[(tq, (tmq, 2 * LANES), lambda g, i, j: (i % nq_t, 0))], out_dtype=BF16, name="mla_uq")

    if ctx_ckv is not None:
        past = ctx_ckv.shape[1]
        ckv_all = jnp.concatenate([ctx_ckv.astype(BF16), ckv.reshape(nseq, L, kv_rank).astype(BF16)], axis=1)
        ck = ctx_kpe.astype(F32)
        ctx_kp = jnp.concatenate([ck, ck, jnp.zeros_like(ck), jnp.zeros_like(ck)], axis=-1)
        kpks_all = jnp.concatenate([ctx_kp, kpks.reshape(nseq, L, 2 * LANES)], axis=1)
        tk_ctx = jnp.concatenate([jnp.ones((past, LANES), F32), jnp.zeros((past, LANES), F32)], axis=-1)
        tk_all = jnp.concatenate([tk_ctx, jnp.concatenate([cos, cos, sin, sin], axis=-1)], axis=0)
        Lk = past + L
    else:
        ckv_all = ckv.reshape(nseq, L, kv_rank).astype(BF16)
        kpks_all = kpks.reshape(nseq, L, 2 * LANES)
        Lk = L
        tk_all = None
    tmk = _pick(Lk, (512, 256, 128))
    if tk_all is None:
        tk_all = jnp.concatenate([jnp.ones((tmk, LANES), F32), jnp.zeros((tmk, LANES), F32)], axis=-1)
    nk_t = tk_all.shape[0] // tmk
    Mk = nseq * Lk
    ckv_all = ckv_all.reshape(Mk, kv_rank)
    kcat = mm2d(ckv_all, wts['ukn_aug'], epilogue=_epi_kadd, tm=tmk,
                aux=[(kpks_all.reshape(Mk, 2 * LANES), (tmk, 2 * LANES), lambda g, i, j: (i, 0)),
                     (tk_all, (tmk, 2 * LANES), lambda g, i, j: (i % nk_t, 0))],
                out_dtype=BF16, name="mla_ukn")
    v = mm2d(ckv_all, wts['uv'], tm=tmk, out_dtype=BF16, name="mla_uv")
    att = attention(q.reshape(nseq, L, -1), kcat.reshape(nseq, Lk, -1), v.reshape(nseq, Lk, -1), heads)

    Dc = S5_CHUNK
    R = M // Dc
    n = L // Dc
    y_intra = mm(u2, w_t, name="s5_intra")
    g = mm(u2, w_b, name="s5_to_state")
    s_in, fin = s5_scan(g, ad, h0, n)
    tms = _pick(R, (512, 256, 128))
    tns = Dc * LANES
    gyb = mm(s_in, w_c, epilogue=_epi_add_gelu, tm=tms, tn=tns,
             aux=[(y_intra, (1, tms, tns), lambda g_, i, j: (g_, i, j))], out_mode="unfold_rows", fold=Dc,
             name="s5_from_state")
    s5w = nb * LANES
    tmg = _pick(M, (1024, 512, 256, 128))
    tng = _pick(s5w, (1024, 512, 256, 128))
    s5_out = mm(gyb, glu_w.astype(BF16)[None], epilogue=_epi_glu, a_blocked=True, tm=tmg, tn=tng,
                aux=[(gyb, (tng // LANES, tmg, LANES), lambda g_, i, j: (j, i, 0)),
                     (glu_b[None], (1, tng), lambda g_, i, j: (0, j))], out_dtype=BF16, name="s5_glu")[0]

    cat = jnp.concatenate([s5_out, att.reshape(M, -1)], axis=-1)
    y = mm2d(cat, wts['w_out'], out_dtype=BF16, name="ab_out")
    return y.reshape(nseq, L, D), ckv, kpks[:, :64], fin


def _pack_s5_state(re, im, nb):
    nseq = re.shape[0]
    def blk(x):
        return x.reshape(nseq, nb, -1).transpose(1, 0, 2)
    parts = [blk(re[:, 0]), blk(im[:, 0]), blk(re[:, 1]), blk(im[:, 1])]
    return jnp.concatenate(parts, axis=-1)[:, :, None, :].astype(F32)


def _unpack_s5_state(fin, groups, states):
    nb, nseq = fin.shape[0], fin.shape[1]
    half = fin.shape[-1] // 4
    def blk(x):
        return x.transpose(1, 0, 2).reshape(nseq, groups, states)
    f = fin[:, :, 0]
    re = jnp.stack([blk(f[..., 0:half]), blk(f[..., 2 * half:3 * half])], axis=1)
    im = jnp.stack([blk(f[..., half:2 * half]), blk(f[..., 3 * half:])], axis=1)
    return re, im


def _hg_mixer(h, w_in, w_out, j, lb, out_norm, s0, heads):
    nseq, L, D = h.shape
    M = nseq * L
    z = mm2d(h.reshape(M, D), w_in, wg=j, name="hg_in").reshape(nseq, L, -1)
    o_f, o_b, st = hgrn2(z, lb, s0, heads)
    o = hg_post(o_f, o_b, z, out_norm, heads)
    y = mm2d(o.reshape(M, -1), w_out, wg=j, out_dtype=BF16, name="hg_out")
    return y.reshape(nseq, L, D), st


def _mlp(h, w1, w2, layer):
    nseq, L, D = h.shape
    M = nseq * L
    z = mm2d(h.reshape(M, D), w1, wg=layer, epilogue=_epi_relu2, out_dtype=BF16, name="mlp_up")
    return mm2d(z, w2, wg=layer, out_dtype=BF16, name="mlp_down").reshape(nseq, L, D)


def kernel(x_prompt, x_sample, cache_ckv, cache_kpe, state_s5_re, state_s5_im, state_hgrn, c, c_ctx,
           mod_w, mod_b, norm_g, mlp_w1, mlp_w2, ab_w_in, ab_w_out, mla_q_norm, mla_kv_norm, mla_w_uq,
           mla_w_ukv, s5_log_dt, s5_lam_re, s5_lam_im, s5_b_re, s5_b_im, s5_c_re, s5_c_im, s5_d, s5_glu_w,
           s5_glu_b, hg_w_in, hg_w_out, hg_lower_bounds, hg_out_norm):
    depth = mod_w.shape[0]
    D = x_prompt.shape[-1]
    nsmp = x_sample.shape[0]
    s5w = s5_glu_w.shape[-1]
    q_rank = mla_q_norm.shape[-1]
    kv_rank = mla_kv_norm.shape[-1]
    mla_heads = (D - s5w) // 128
    hg_heads = hg_out_norm.shape[-1] // 128
    groups, states = s5_lam_re.shape[2], s5_lam_re.shape[3]

    lbs = jax.nn.softmax(hg_lower_bounds.astype(F32), axis=1)
    lbs = jnp.cumsum(lbs, axis=1) - lbs[:, :1]

    n_cond = nsmp + 1
    pad = (-n_cond) % 8
    cond = jnp.concatenate([c, c_ctx[None], jnp.zeros((pad, D), F32)], axis=0)

    mods = [modulation(cond, mod_w, mod_b, layer) for layer in range(depth)]
    w1_all, w2_all = mlp_w1.astype(BF16), mlp_w2.astype(BF16)
    hg_in_all, hg_out_all = hg_w_in.astype(BF16), hg_w_out.astype(BF16)

    xp, xs = x_prompt, x_sample
    hp = hs = None
    l_ckv, l_kpe, l_s5r, l_s5i, l_hg = [], [], [], [], []
    for layer in range(depth):
        j = layer // 2
        m = mods[layer]
        ms = [m[:nsmp, i * D:(i + 1) * D][:, None, :] for i in range(6)]
        mp = [m[nsmp:nsmp + 1, i * D:(i + 1) * D][:, None, :] for i in range(6)]
        if layer == 0:
            hp = norm_mod(xp, norm_g[layer, 0], mp[1], mp[0])
            hs = norm_mod(xs, norm_g[layer, 0], ms[1], ms[0])
        if layer % 2 == 0:
            wts = _ab_weights(ab_w_in[j], ab_w_out[j], mla_q_norm[j], mla_kv_norm[j], mla_w_uq[j], mla_w_ukv[j],
                              s5w, q_rank, kv_rank, mla_heads)
            s5m = _s5_weights(s5_log_dt[j], s5_lam_re[j], s5_lam_im[j], s5_b_re[j], s5_b_im[j],
                              s5_c_re[j], s5_c_im[j], s5_d[j])
            nb = s5m[0].shape[0]
            zero_h0 = jnp.zeros((nb, xp.shape[0], 1, s5m[3].shape[-1]), F32)
            yp, ckv, kpe, fin = _ab_mixer(hp, wts, s5m, s5_glu_w[j], s5_glu_b[j], heads=mla_heads,
                                          kv_rank=kv_rank, ctx_ckv=None, ctx_kpe=None, h0=zero_h0, rope=False)
            h0s = _pack_s5_state(state_s5_re[:, j], state_s5_im[:, j], nb)
            ys, _, _, _ = _ab_mixer(hs, wts, s5m, s5_glu_w[j], s5_glu_b[j], heads=mla_heads, kv_rank=kv_rank,
                                    ctx_ckv=cache_ckv[:, j], ctx_kpe=cache_kpe[:, j], h0=h0s, rope=True)
            hr, hi = _unpack_s5_state(fin, groups, states)
            l_ckv.append(ckv.reshape(xp.shape[0], xp.shape[1], kv_rank))
            l_kpe.append(kpe.reshape(xp.shape[0], xp.shape[1], 64))
            l_s5r.append(hr)
            l_s5i.append(hi)
        else:
            yp, st = _hg_mixer(hp, hg_in_all, hg_out_all, j, lbs[:, layer], hg_out_norm[j], None, hg_heads)
            ys, _ = _hg_mixer(hs, hg_in_all, hg_out_all, j, lbs[:, layer], hg_out_norm[j], state_hgrn[:, j],
                              hg_heads)
            l_hg.append(st)
        xp, hp = resid_norm(xp, yp, norm_g[layer, 1], mp[2], (norm_g[layer, 2], mp[4], mp[3]))
        xs, hs = resid_norm(xs, ys, norm_g[layer, 1], ms[2], (norm_g[layer, 2], ms[4], ms[3]))
        yp = _mlp(hp, w1_all, w2_all, layer)
        ys = _mlp(hs, w1_all, w2_all, layer)
        if layer + 1 < depth:
            m_n = mods[layer + 1]
            nxt_s = (norm_g[layer + 1, 0], m_n[:nsmp, D:2 * D][:, None, :], m_n[:nsmp, 0:D][:, None, :])
            nxt_p = (norm_g[layer + 1, 0], m_n[nsmp:nsmp + 1, D:2 * D][:, None, :],
                     m_n[nsmp:nsmp + 1, 0:D][:, None, :])
        else:
            nxt_s = nxt_p = None
        xp, hp = resid_norm(xp, yp, norm_g[layer, 3], mp[5], nxt_p)
        xs, hs = resid_norm(xs, ys, norm_g[layer, 3], ms[5], nxt_s)
    new_ckv = jnp.stack(l_ckv, axis=1)
    new_kpe = jnp.stack(l_kpe, axis=1)
    new_s5_re = jnp.stack(l_s5r, axis=1)
    new_s5_im = jnp.stack(l_s5i, axis=1)
    new_hgrn = jnp.stack(l_hg, axis=1)
    return (xp, xs, new_ckv, new_kpe, new_s5_re, new_s5_im, new_hgrn)
```

```python
import functools
import math

import jax
import jax.numpy as jnp
from jax import lax
from jax.experimental import pallas as pl
from jax.experimental.pallas import tpu as pltpu

F32 = jnp.float32
BF16 = jnp.bfloat16

EPS = 1e-6
GRID_W = 64
ROPE_BASE = 10000.0
LANES = 128
S5_CHUNK = 16
HG_SUB = 32
HG_BLOCK = 2 * HG_SUB
VMEM_LIMIT = 56 * 1024 * 1024
MM_VMEM_BUDGET = 42 * 1024 * 1024
ATT_HEADS = 2
HG_POST_HEADS = 8
HG_HEADS = 4
HG_BLOCKS = 16
S5_SEQS = 4


def _pick(n, prefs):
    for p in prefs:
        if n % p == 0:
            return p
    return n


def _cparams(sem):
    return pltpu.CompilerParams(dimension_semantics=sem, vmem_limit_bytes=VMEM_LIMIT)


def _sigmoid(x):
    return 0.5 + 0.5 * jnp.tanh(0.5 * x)


def _silu(x):
    return x * _sigmoid(x)


def _gelu_tanh(x):
    c = math.sqrt(2.0 / math.pi)
    return 0.5 * x * (1.0 + jnp.tanh(c * (x + 0.044715 * (x * x * x))))


def _rms(x):
    return x * lax.rsqrt(jnp.mean(x * x, axis=-1, keepdims=True) + EPS)


def _mm_kernel(*refs, nk, n_aux, epilogue, a_blocked, out_mode, fold):
    a_ref, w_ref = refs[0], refs[1]
    aux_refs = refs[2:2 + n_aux]
    o_ref = refs[2 + n_aux]
    scratch = list(refs[3 + n_aux:])

    def load_a():
        if a_blocked:
            return jnp.concatenate([a_ref[c] for c in range(a_ref.shape[0])], axis=-1).astype(BF16)
        return a_ref[0].astype(BF16)

    def finish(acc):
        aux = []
        for r in aux_refs:
            v = r[...]
            aux.append(v)
        out = epilogue(acc, *aux) if epilogue is not None else acc
        if out_mode == "fold_rows":
            slab_ref = scratch.pop()
            rows = slab_ref.shape[1] // fold
            for c in range(o_ref.shape[0]):
                slab_ref[c] = out[:, c * LANES:(c + 1) * LANES]
            for c in range(o_ref.shape[0]):
                for t in range(fold):
                    o_ref[c, :, t * LANES:(t + 1) * LANES] = (
                        slab_ref[c, pl.ds(t, rows, stride=fold), :].astype(o_ref.dtype))
        elif out_mode == "unfold_rows":
            rows = out.shape[0]
            for t in range(fold):
                o_ref[0, pl.ds(t, rows, stride=fold), :] = out[:, t * LANES:(t + 1) * LANES].astype(o_ref.dtype)
        else:
            o_ref[0] = out.astype(o_ref.dtype)

    if nk == 1:
        finish(jnp.dot(load_a(), w_ref[0].astype(BF16), preferred_element_type=F32))
        return

    acc_ref = scratch.pop(0)
    k = pl.program_id(3)

    @pl.when(k == 0)
    def _():
        acc_ref[...] = jnp.zeros_like(acc_ref)

    acc_ref[...] += jnp.dot(load_a(), w_ref[0].astype(BF16), preferred_element_type=F32)

    @pl.when(k == nk - 1)
    def _():
        finish(acc_ref[...])


def _pick_tk(K, tm, tn, a_bytes, w_bytes, out_bytes, aux_bytes):
    for tk in (K, 4096, 2048, 1024, 512, 256, 128):
        if tk > K or K % tk:
            continue
        acc = 0 if tk == K else 4 * tm * tn
        est = 2 * (tm * tk * a_bytes + tk * tn * w_bytes) + 2 * tm * tn * out_bytes + acc + 2 * aux_bytes
        if est <= MM_VMEM_BUDGET:
            return tk
    return LANES


def mm(a, w, *, wg=0, epilogue=None, aux=(), out_dtype=F32, tm=None, tn=None, tk=None,
       a_blocked=False, out_mode="plain", fold=1, name="mm"):
    _, K, N = w.shape
    G = 1 if a_blocked else a.shape[0]
    M = a.shape[1]
    tm = tm or _pick(M, (1024, 512, 256, 128))
    tn = tn or _pick(N, (1024, 512, 256, 128))
    if tk is None:
        aux_bytes = sum(math.prod(bshape) * arr.dtype.itemsize for arr, bshape, _ in aux)
        tk = _pick_tk(K, tm, tn, a.dtype.itemsize, w.dtype.itemsize, jnp.dtype(out_dtype).itemsize, aux_bytes)
    assert M % tm == 0 and N % tn == 0 and K % tk == 0, (M, N, K, tm, tn, tk)
    nk = K // tk
    grid = (G, M // tm, N // tn, nk)
    if a_blocked:
        a_spec = pl.BlockSpec((tk // LANES, tm, LANES), lambda g, i, j, k: (k, i, 0))
    else:
        a_spec = pl.BlockSpec((1, tm, tk), lambda g, i, j, k: (g, i, k))
    w_spec = pl.BlockSpec((1, tk, tn), lambda g, i, j, k: (wg + g, k, j))
    aux_arrays, aux_specs = [], []
    for arr, bshape, imap in aux:
        aux_arrays.append(arr)
        aux_specs.append(pl.BlockSpec(bshape, functools.partial(lambda g, i, j, k, f: f(g, i, j), f=imap)))
    scratch = [] if nk == 1 else [pltpu.VMEM((tm, tn), F32)]
    if out_mode == "fold_rows":
        assert G == 1 and tm % fold == 0
        out_shape = jax.ShapeDtypeStruct((N // LANES, M // fold, fold * LANES), out_dtype)
        out_spec = pl.BlockSpec((tn // LANES, tm // fold, fold * LANES), lambda g, i, j, k: (j, i, 0))
        scratch.append(pltpu.VMEM((tn // LANES, tm, LANES), F32))
    elif out_mode == "unfold_rows":
        assert tn == N == fold * LANES
        out_shape = jax.ShapeDtypeStruct((G, M * fold, LANES), out_dtype)
        out_spec = pl.BlockSpec((1, tm * fold, LANES), lambda g, i, j, k: (g, i, 0))
    else:
        out_shape = jax.ShapeDtypeStruct((G, M, N), out_dtype)
        out_spec = pl.BlockSpec((1, tm, tn), lambda g, i, j, k: (g, i, j))
    kern = functools.partial(_mm_kernel, nk=nk, n_aux=len(aux_arrays), epilogue=epilogue,
                             a_blocked=a_blocked, out_mode=out_mode, fold=fold)
    return pl.pallas_call(
        kern, grid=grid, in_specs=[a_spec, w_spec] + aux_specs, out_specs=out_spec,
        out_shape=out_shape, scratch_shapes=scratch, name=name,
        compiler_params=_cparams(("parallel", "parallel", "parallel", "arbitrary")),
    )(a, w, *aux_arrays)


def mm2d(a, w, **kw):
    out = mm(a[None], w if w.ndim == 3 else w[None], **kw)
    return out if kw.get("out_mode", "plain") != "plain" else out[0]


def _epi_relu2(acc):
    r = jnp.maximum(acc, 0.0)
    return r * r


def _epi_rms(acc, g):
    return _rms(acc) * g


def _epi_ckv(acc, g, *, rank):
    return jnp.concatenate([_rms(acc[:, :rank]) * g, acc[:, rank:]], axis=-1)


def _epi_table(acc, t):
    reps = acc.shape[1] // t.shape[1]
    return acc * jnp.concatenate([t] * reps, axis=-1)


def _epi_kadd(acc, kpks, t):
    kr2 = kpks[:, :LANES] * t[:, :LANES] + kpks[:, LANES:] * t[:, LANES:]
    blk = jnp.concatenate([jnp.zeros_like(kr2), kr2], axis=-1)
    reps = acc.shape[1] // blk.shape[1]
    return acc + jnp.concatenate([blk] * reps, axis=-1)


def _epi_add_gelu(acc, y0):
    return _gelu_tanh(acc + y0[0])


def _epi_glu(acc, gy, b):
    y = jnp.concatenate([gy[c] for c in range(gy.shape[0])], axis=-1).astype(F32)
    return y * _sigmoid(acc + b)


def _mod_kernel(c_ref, w_ref, b_ref, o_ref, *, nk):
    k = pl.program_id(1)

    @pl.when(k == 0)
    def _():
        o_ref[...] = jnp.zeros_like(o_ref)

    a = _silu(c_ref[...]).astype(BF16)
    o_ref[...] += jnp.dot(a, w_ref[...].astype(BF16), preferred_element_type=F32)

    @pl.when(k == nk - 1)
    def _():
        o_ref[...] += b_ref[...]


def modulation(cond, w, b, layer):
    rows, d = cond.shape
    n = w.shape[2]
    tn = _pick(n, (2048, 1024, 512, 256, 128))
    tk = _pick(d, (1024, 512, 256, 128))
    nk = d // tk
    return pl.pallas_call(
        functools.partial(_mod_kernel, nk=nk), grid=(n // tn, nk),
        in_specs=[pl.BlockSpec((rows, tk), lambda j, k: (0, k)),
                  pl.BlockSpec((None, tk, tn), lambda j, k: (layer, k, j)),
                  pl.BlockSpec((None, 1, tn), lambda j, k: (layer, 0, j))],
        out_specs=pl.BlockSpec((rows, tn), lambda j, k: (0, j)),
        out_shape=jax.ShapeDtypeStruct((rows, n), F32), name="modulation",
        compiler_params=_cparams(("parallel", "arbitrary")),
    )(cond, w, b[:, None, :])


def _norm_mod_kernel(x_ref, g_ref, sc_ref, sh_ref, o_ref):
    y = _rms(x_ref[0]) * g_ref[...]
    o_ref[0] = (y * (1.0 + sc_ref[0]) + sh_ref[0]).astype(o_ref.dtype)


def norm_mod(x, g, sc, sh):
    nseq, L, D = x.shape
    tr = _pick(L, (256, 128))
    per_seq = sc.shape[0] == nseq and nseq > 1
    smap = (lambda b, i: (b, 0, 0)) if per_seq else (lambda b, i: (0, 0, 0))
    return pl.pallas_call(
        _norm_mod_kernel, grid=(nseq, L // tr),
        in_specs=[pl.BlockSpec((1, tr, D), lambda b, i: (b, i, 0)),
                  pl.BlockSpec((1, D), lambda b, i: (0, 0)),
                  pl.BlockSpec((1, 1, D), smap), pl.BlockSpec((1, 1, D), smap)],
        out_specs=pl.BlockSpec((1, tr, D), lambda b, i: (b, i, 0)),
        out_shape=jax.ShapeDtypeStruct((nseq, L, D), BF16), name="norm_mod",
        compiler_params=_cparams(("parallel", "parallel")),
    )(x, g[None], sc, sh)


def _resid_kernel(x_ref, y_ref, g1_ref, gt_ref, *rest, with_h):
    xn = x_ref[0] + gt_ref[0] * (_rms(y_ref[0].astype(F32)) * g1_ref[...])
    if with_h:
        g2_ref, sc_ref, sh_ref, xo_ref, h_ref = rest
        xo_ref[0] = xn
        h_ref[0] = ((_rms(xn) * g2_ref[...]) * (1.0 + sc_ref[0]) + sh_ref[0]).astype(h_ref.dtype)
    else:
        (xo_ref,) = rest
        xo_ref[0] = xn


def resid_norm(x, y, g1, gate, nxt=None):
    nseq, L, D = x.shape
    tr = _pick(L, (256, 128))
    per_seq = gate.shape[0] == nseq and nseq > 1
    smap = (lambda b, i: (b, 0, 0)) if per_seq else (lambda b, i: (0, 0, 0))
    row = pl.BlockSpec((1, tr, D), lambda b, i: (b, i, 0))
    vec = pl.BlockSpec((1, D), lambda b, i: (0, 0))
    mod = pl.BlockSpec((1, 1, D), smap)
    args = [x, y, g1[None], gate]
    specs = [row, row, vec, mod]
    out_shape = [jax.ShapeDtypeStruct((nseq, L, D), F32)]
    out_specs = [row]
    if nxt is not None:
        g2, sc, sh = nxt
        args += [g2[None], sc, sh]
        specs += [vec, mod, mod]
        out_shape.append(jax.ShapeDtypeStruct((nseq, L, D), BF16))
        out_specs.append(row)
    res = pl.pallas_call(
        functools.partial(_resid_kernel, with_h=nxt is not None), grid=(nseq, L // tr),
        in_specs=specs, out_specs=out_specs, out_shape=out_shape, name="resid_norm",
        compiler_params=_cparams(("parallel", "parallel")),
    )(*args)
    return (res[0], res[1]) if nxt is not None else (res[0], None)


def _attn_kernel(q_ref, k_ref, v_ref, o_ref, *, hp):
    scores = []
    for h in range(hp):
        qk = slice(h * 2 * LANES, (h + 1) * 2 * LANES)
        scores.append(lax.dot_general(q_ref[0, :, qk], k_ref[0, :, qk], (((1,), (1,)), ((), ())),
                                      preferred_element_type=F32))
    probs = []
    for s in scores:
        m = jnp.max(s, axis=-1, keepdims=True)
        p = jnp.exp2(s - m)
        probs.append((p.astype(BF16), jnp.sum(p, axis=-1, keepdims=True)))
    for h, (p, l) in enumerate(probs):
        vo = slice(h * LANES, (h + 1) * LANES)
        o = jnp.dot(p, v_ref[0, :, vo], preferred_element_type=F32)
        o_ref[0, :, vo] = (o / l).astype(o_ref.dtype)


def attention(q, kcat, v, heads):
    nseq, L, _ = q.shape
    Lk = kcat.shape[1]
    tq = _pick(L, (256, 128))
    hp = ATT_HEADS if heads % ATT_HEADS == 0 else 1
    return pl.pallas_call(
        functools.partial(_attn_kernel, hp=hp), grid=(nseq, heads // hp, L // tq),
        in_specs=[pl.BlockSpec((1, tq, hp * 2 * LANES), lambda b, h, i: (b, i, h)),
                  pl.BlockSpec((1, Lk, hp * 2 * LANES), lambda b, h, i: (b, 0, h)),
                  pl.BlockSpec((1, Lk, hp * LANES), lambda b, h, i: (b, 0, h))],
        out_specs=pl.BlockSpec((1, tq, hp * LANES), lambda b, h, i: (b, i, h)),
        out_shape=jax.ShapeDtypeStruct((nseq, L, heads * LANES), BF16), name="mla_attention",
        compiler_params=_cparams(("parallel", "parallel", "arbitrary")),
    )(q, kcat, v)


def _s5_scan_kernel(g_ref, ad_ref, h0_ref, sin_ref, fin_ref, *, n, sb, half):
    ad = ad_ref[0]
    afr, afi = ad[:, 0:half], ad[:, half:2 * half]
    abr, abi = ad[:, 2 * half:3 * half], ad[:, 3 * half:4 * half]

    def body(k, carry):
        out = []
        for s in range(sb):
            fr, fi, br, bi = carry[4 * s:4 * s + 4]
            row_f = pl.ds(s * n + k, 1)
            row_b = pl.ds(s * n + n - 1 - k, 1)
            sin_ref[0, row_f, 0:2 * half] = jnp.concatenate([fr, fi], axis=-1)
            sin_ref[0, row_b, 2 * half:4 * half] = jnp.concatenate([br, bi], axis=-1)
            gf = g_ref[0, row_f, 0:2 * half]
            gb = g_ref[0, row_b, 2 * half:4 * half]
            out += [afr * fr - afi * fi + gf[:, :half], afr * fi + afi * fr + gf[:, half:],
                    abr * br - abi * bi + gb[:, :half], abr * bi + abi * br + gb[:, half:]]
        return tuple(out)

    init = []
    for s in range(sb):
        h0 = h0_ref[0, s]
        init += [h0[:, 0:half], h0[:, half:2 * half], h0[:, 2 * half:3 * half], h0[:, 3 * half:4 * half]]
    fin = lax.fori_loop(0, n, body, tuple(init))
    for s in range(sb):
        fin_ref[0, s] = jnp.concatenate(fin[4 * s:4 * s + 4], axis=-1)


def s5_scan(g, ad, h0, n):
    nb, rows, W = g.shape
    nseq = rows // n
    sb = S5_SEQS if nseq % S5_SEQS == 0 else 1
    return pl.pallas_call(
        functools.partial(_s5_scan_kernel, n=n, sb=sb, half=W // 4), grid=(nb, nseq // sb),
        in_specs=[pl.BlockSpec((1, sb * n, W), lambda j, b: (j, b, 0)),
                  pl.BlockSpec((1, 1, W), lambda j, b: (j, 0, 0)),
                  pl.BlockSpec((1, sb, 1, W), lambda j, b: (j, b, 0, 0))],
        out_specs=[pl.BlockSpec((1, sb * n, W), lambda j, b: (j, b, 0)),
                   pl.BlockSpec((1, sb, 1, W), lambda j, b: (j, b, 0, 0))],
        out_shape=[jax.ShapeDtypeStruct((nb, rows, W), F32),
                   jax.ShapeDtypeStruct((nb, nseq, 1, W), F32)], name="s5_scan",
        compiler_params=_cparams(("parallel", "parallel")),
    )(g, ad, h0)


def _hg_masks(rev):
    T = HG_BLOCK
    row = lax.broadcasted_iota(jnp.int32, (T, LANES), 0)
    first = row < HG_SUB
    r2 = lax.broadcasted_iota(jnp.int32, (T, 2 * T), 0)
    c2 = lax.broadcasted_iota(jnp.int32, (T, 2 * T), 1)
    s2 = c2 & (T - 1)
    sub_bits = HG_SUB.bit_length() - 1
    other_sub = (r2 ^ s2) >> sub_bits
    causal = (s2 >= r2) if rev else (s2 <= r2)
    keep = jnp.where(other_sub == (c2 >> (sub_bits + 1)), jnp.where(causal, 1, 0), 0) > 0
    rin = row & (HG_SUB - 1)
    steps = (1, 2, 4, 8, 16)
    scan = [(rin < HG_SUB - s) if rev else (rin >= s) for s in steps]
    return steps, scan, first, keep


def _hg_direction(q, z, v, lb, st, rev, masks):
    T = HG_BLOCK
    steps, scan, first, keep = masks
    qa = _silu(q) * (LANES ** -0.5)
    th = 0.5 * jnp.tanh(0.5 * z)
    sig = 0.5 + th
    nsig = 0.5 - th
    kk = (1.0 - lb) * nsig
    g = jnp.log(lb + (1.0 - lb) * sig)

    b = g
    for s, ok in zip(steps, scan):
        b = b + jnp.where(ok, pltpu.roll(b, (T - s) if rev else s, axis=0), 0.0)
    if not rev:
        b0, b1 = b[HG_SUB - 1:HG_SUB], b[T - 1:T]
    else:
        b0, b1 = b[0:1], b[HG_SUB:HG_SUB + 1]
    bsub = jnp.where(first, b0, b1)
    qh = qa * jnp.exp(b)
    kd = kk * jnp.exp(-b)
    ke = kk * jnp.exp(bsub - b)

    kcat = jnp.concatenate([kd, ke], axis=0).astype(BF16)
    att = lax.dot_general(qh.astype(BF16), kcat, (((1,), (1,)), ((), ())), preferred_element_type=F32)
    att = jnp.where(keep, att, 0.0)
    att = att + pltpu.roll(att, T, axis=1)
    vb = v.astype(BF16)
    o = jnp.dot(att[:, :T].astype(BF16), vb, preferred_element_type=F32)

    if not rev:
        dq = jnp.where(first, 1.0, jnp.exp(b0))
        ek = jnp.where(first, jnp.exp(b1), 1.0)
    else:
        dq = jnp.where(first, jnp.exp(b1), 1.0)
        ek = jnp.where(first, 1.0, jnp.exp(b0))
    o = o + lax.dot_general((qh * dq).astype(BF16), st.astype(BF16), (((1,), (1,)), ((), ())),
                            preferred_element_type=F32)
    upd = lax.dot_general(vb, (ke * ek).astype(BF16), (((0,), (0,)), ((), ())), preferred_element_type=F32)
    st_new = st * jnp.exp(b0 + b1) + upd
    return o, st_new


def _hg_kernel(*refs, nsteps, nb, hp, has_init):
    if has_init:
        (qf_ref, zf_ref, vf_ref, qb_ref, zb_ref, vb_ref, lb_ref, s0_ref,
         of_ref, ob_ref, so_ref, stf_ref, stb_ref) = refs
    else:
        (qf_ref, zf_ref, vf_ref, qb_ref, zb_ref, vb_ref, lb_ref,
         of_ref, ob_ref, so_ref, stf_ref, stb_ref) = refs
    i = pl.program_id(2)
    T = HG_BLOCK

    @pl.when(i == 0)
    def _():
        for h in range(hp):
            if has_init:
                stf_ref[h] = s0_ref[0, 0, h].T
                stb_ref[h] = s0_ref[0, 1, h].T
            else:
                stf_ref[h] = jnp.zeros((LANES, LANES), F32)
                stb_ref[h] = jnp.zeros((LANES, LANES), F32)

    lb = lb_ref[...]
    masks_f = _hg_masks(False)
    masks_b = _hg_masks(True)
    for h in range(hp):
        lanes = slice(h * LANES, (h + 1) * LANES)
        st_f = stf_ref[h]
        st_b = stb_ref[h]
        for blk in range(nb):
            rows = slice(blk * T, (blk + 1) * T)
            o_f, st_f = _hg_direction(qf_ref[0, rows, lanes], zf_ref[0, rows, lanes], vf_ref[0, rows, lanes],
                                      lb[0:1, lanes], st_f, False, masks_f)
            of_ref[0, rows, lanes] = o_f.astype(of_ref.dtype)
        for blk in reversed(range(nb)):
            rows = slice(blk * T, (blk + 1) * T)
            o_b, st_b = _hg_direction(qb_ref[0, rows, lanes], zb_ref[0, rows, lanes], vb_ref[0, rows, lanes],
                                      lb[1:2, lanes], st_b, True, masks_b)
            ob_ref[0, rows, lanes] = o_b.astype(ob_ref.dtype)
        stf_ref[h] = st_f
        stb_ref[h] = st_b

        @pl.when(i == nsteps - 1)
        def _(h=h, st_f=st_f, st_b=st_b):
            so_ref[0, 0, h] = st_f.T
            so_ref[0, 1, h] = st_b.T


def hgrn2(z, lb, s0, heads):
    nseq, L, _ = z.shape
    hp = HG_HEADS if heads % HG_HEADS == 0 else 1
    nb = max(n for n in range(1, HG_BLOCKS + 1) if L % (n * HG_BLOCK) == 0)
    T = nb * HG_BLOCK
    nsteps = L // T
    HB = heads // hp

    def col(off, rev):
        if rev:
            return lambda b, h, i: (b, nsteps - 1 - i, off + h)
        return lambda b, h, i: (b, i, off + h)

    tile = lambda off, rev: pl.BlockSpec((1, T, hp * LANES), col(off, rev))
    in_specs = [tile(0, False), tile(HB, False), tile(3 * HB, False),
                tile(0, True), tile(2 * HB, True), tile(3 * HB, True),
                pl.BlockSpec((2, hp * LANES), lambda b, h, i: (0, h))]
    args = [z, z, z, z, z, z, lb]
    st_spec = pl.BlockSpec((1, 2, hp, LANES, LANES), lambda b, h, i: (b, 0, h, 0, 0))
    if s0 is not None:
        in_specs.append(st_spec)
        args.append(s0)
    o_shape = jax.ShapeDtypeStruct((nseq, L, heads * LANES), BF16)
    st_scratch = pltpu.VMEM((hp, LANES, LANES), F32)
    return pl.pallas_call(
        functools.partial(_hg_kernel, nsteps=nsteps, nb=nb, hp=hp, has_init=s0 is not None),
        grid=(nseq, HB, nsteps), in_specs=in_specs,
        out_specs=[tile(0, False), tile(0, True), st_spec],
        out_shape=[o_shape, o_shape, jax.ShapeDtypeStruct((nseq, 2, heads, LANES, LANES), F32)],
        scratch_shapes=[st_scratch, st_scratch], name="hgrn2",
        compiler_params=_cparams(("parallel", "parallel", "arbitrary")),
    )(*args)


def _hg_post_kernel(of_ref, ob_ref, g_ref, gn_ref, o_ref, *, hp):
    for h in range(hp):
        lanes = slice(h * LANES, (h + 1) * LANES)
        o = of_ref[0, :, lanes].astype(F32) + ob_ref[0, :, lanes].astype(F32)
        o_ref[0, :, lanes] = ((_rms(o) * gn_ref[:, lanes]) * _silu(g_ref[0, :, lanes])).astype(o_ref.dtype)


def hg_post(o_f, o_b, z, out_norm, heads):
    nseq, L, _ = o_f.shape
    tr = _pick(L, (256, 128))
    hp = HG_POST_HEADS if heads % HG_POST_HEADS == 0 else 1
    tile = pl.BlockSpec((1, tr, hp * LANES), lambda b, i, h: (b, i, h))
    return pl.pallas_call(
        functools.partial(_hg_post_kernel, hp=hp), grid=(nseq, L // tr, heads // hp),
        in_specs=[tile, tile, pl.BlockSpec((1, tr, hp * LANES), lambda b, i, h: (b, i, 4 * (heads // hp) + h)),
                  pl.BlockSpec((1, hp * LANES), lambda b, i, h: (0, h))],
        out_specs=tile, out_shape=jax.ShapeDtypeStruct(o_f.shape, BF16), name="hg_post",
        compiler_params=_cparams(("parallel", "parallel", "parallel")),
    )(o_f, o_b, z, out_norm[None])


def _rope_tables(n_l):
    rope = 64
    half = rope // 2
    rows = n_l // GRID_W
    row = jnp.repeat(jnp.arange(rows), GRID_W).astype(F32)
    col = jnp.tile(jnp.arange(GRID_W), rows).astype(F32)
    inv = ROPE_BASE ** (-jnp.arange(0, half, 2, dtype=F32) / half)
    ar = row[:, None] * inv[None]
    ac = col[:, None] * inv[None]
    cos = jnp.concatenate([jnp.cos(ar), jnp.cos(ar), jnp.cos(ac), jnp.cos(ac)], axis=-1)
    sin = jnp.concatenate([jnp.sin(ar), jnp.sin(ar), jnp.sin(ac), jnp.sin(ac)], axis=-1)
    return cos, sin


def _rot_cols(w):
    return jnp.concatenate([-w[..., 16:32], w[..., 0:16], -w[..., 48:64], w[..., 32:48]], axis=-1)


def _s5_weights(log_dt, lam_re, lam_im, b_re, b_im, c_re, c_im, d_skip):
    D = S5_CHUNK
    G, P = lam_re.shape[1], lam_re.shape[2]
    C = b_re.shape[-1]
    gl = LANES // C
    nb = G // gl
    dt = jnp.exp(log_dt.astype(F32))[..., None]
    lr, li = lam_re.astype(F32), lam_im.astype(F32)
    mag = jnp.exp(lr * dt)
    ar, ai = mag * jnp.cos(li * dt), mag * jnp.sin(li * dt)
    den = lr * lr + li * li
    cr_ = ((ar - 1.0) * lr + ai * li) / den
    ci_ = (ai * lr - (ar - 1.0) * li) / den
    br, bi = b_re.astype(F32), b_im.astype(F32)
    bbr = cr_[..., None] * br - ci_[..., None] * bi
    bbi = cr_[..., None] * bi + ci_[..., None] * br
    cr, ci = c_re.astype(F32), c_im.astype(F32)
    def powers(n):
        n = n.astype(F32)[:, None, None, None]
        pmag = jnp.exp(n * (lr * dt)[None])
        return pmag * jnp.cos(n * (li * dt)[None]), pmag * jnp.sin(n * (li * dt)[None])

    def times_b(p_r, p_i):
        return (p_r[..., None] * bbr[None] - p_i[..., None] * bbi[None],
                p_r[..., None] * bbi[None] + p_i[..., None] * bbr[None])

    steps = jnp.arange(D)
    pr, pi = powers(jnp.arange(D + 1))
    abr, abi = times_b(pr[:D], pi[:D])
    abr_dn, abi_dn = times_b(*powers(D - 1 - steps))
    pr_dn, pi_dn = powers(D - steps)
    half = gl * P
    lane_group = jnp.arange(LANES) // C
    tok_lane_group = jnp.tile(lane_group, D)
    state_group = jnp.arange(half) // P

    def response(x, a_r, a_i):
        return jnp.einsum('gcp,ngpk->ngkc', cr[x], a_r) - jnp.einsum('gcp,ngpk->ngkc', ci[x], a_i)

    kf = response(0, abr[:, 0], abi[:, 0])
    kb = response(1, abr_dn[:, 1], abi_dn[:, 1])
    skip = d_skip.astype(F32)[:, :, None] * jnp.eye(C, dtype=F32)[None]
    zpad = jnp.zeros((D - 1,) + kf.shape[1:], F32)
    lagk = (jnp.concatenate([zpad, kf], axis=0) + jnp.concatenate([kb, zpad], axis=0)
            + jnp.concatenate([zpad, skip[None], zpad], axis=0))
    bd = jnp.tile(lagk.reshape(2 * D - 1, nb, LANES, C), (1, 1, 1, gl))
    bd = jnp.where(lane_group[:, None] == lane_group[None, :], bd, 0.0).astype(BF16)
    lag_idx = jnp.arange(D)[None, :] - jnp.arange(D)[:, None] + (D - 1)
    w_t = bd[lag_idx].transpose(2, 0, 3, 1, 4).reshape(nb, D * LANES, D * LANES)

    def to_state(x):
        rows = x.reshape(D, nb, gl, P, C).transpose(1, 0, 2, 4, 3).reshape(nb, D * LANES, P)
        tiled = jnp.tile(rows, (1, 1, gl))
        return jnp.where(tok_lane_group[:, None] == state_group[None, :], tiled, 0.0).astype(BF16)

    w_b = jnp.concatenate([to_state(abr_dn[:, 0]), to_state(abi_dn[:, 0]),
                           to_state(abr[:, 1]), to_state(abi[:, 1])], axis=-1)

    def from_state(y):
        cols = y.reshape(D, nb, gl, C, P).transpose(1, 4, 0, 2, 3).reshape(nb, P, D * LANES)
        tiled = jnp.tile(cols, (1, gl, 1))
        return jnp.where(state_group[:, None] == tok_lane_group[None, :], tiled, 0.0).astype(BF16)

    def c_times(x, p_r, p_i):
        re = cr[x][None] * p_r[:, :, None, :] - ci[x][None] * p_i[:, :, None, :]
        im = cr[x][None] * p_i[:, :, None, :] + ci[x][None] * p_r[:, :, None, :]
        return re, -im

    cf = c_times(0, pr[1:D + 1, 0], pi[1:D + 1, 0])
    cb = c_times(1, pr_dn[:, 1], pi_dn[:, 1])
    w_c = jnp.concatenate([from_state(cf[0]), from_state(cf[1]), from_state(cb[0]), from_state(cb[1])], axis=1)
    adr = pr[D].reshape(2, nb, half)
    adi = pi[D].reshape(2, nb, half)
    ad = jnp.concatenate([adr[0], adi[0], adr[1], adi[1]], axis=-1)[:, None, :]
    return w_t, w_b, w_c, ad


def _ab_weights(w_in, w_out, q_norm, kv_norm, w_uq, w_ukv, s5w, q_rank, kv_rank, heads):
    rope, nope, vdim = 64, 128, 128
    o1, o2, o3 = s5w, s5w + q_rank, s5w + q_rank + kv_rank
    w_u = w_in[:, :o1].astype(BF16)
    w_q = w_in[:, o1:o2].astype(BF16)
    w_kp = w_in[:, o3:]
    w_ks = _rot_cols(w_kp)
    w_small = jnp.concatenate([w_in[:, o2:o3], w_kp, w_kp, w_ks, w_ks], axis=-1).astype(BF16)
    uq = w_uq.reshape(q_rank, heads, nope + rope)
    uq_aug = jnp.concatenate([uq[..., :nope], uq[..., nope:], _rot_cols(uq[..., nope:])], axis=-1)
    uq_aug = uq_aug.reshape(q_rank, heads * 2 * LANES).astype(BF16)
    ukv = w_ukv.reshape(kv_rank, heads, nope + vdim)
    ukn_aug = jnp.concatenate([ukv[..., :nope], jnp.zeros_like(ukv[..., :nope])], axis=-1)
    ukn_aug = ukn_aug.reshape(kv_rank, heads * 2 * LANES).astype(BF16)
    uv = ukv[..., nope:].reshape(kv_rank, heads * vdim).astype(BF16)
    return dict(w_u=w_u, w_q=w_q, w_small=w_small, uq_aug=uq_aug, ukn_aug=ukn_aug, uv=uv,
                w_out=w_out.astype(BF16), q_norm=q_norm[None], kv_norm=kv_norm[None])


def _ab_mixer(h, wts, s5m, glu_w, glu_b, *, heads, kv_rank, ctx_ckv, ctx_kpe, h0, rope):
    nseq, L, D = h.shape
    M = nseq * L
    h2 = h.reshape(M, D)
    w_t, w_b, w_c, ad = s5m
    nb = w_t.shape[0]
    scale = (128 + 64) ** -0.5 * math.log2(math.e)

    u2 = mm2d(h2, wts['w_u'], out_dtype=BF16, out_mode="fold_rows", fold=S5_CHUNK, name="ab_in_u")
    tmq = _pick(L, (1024, 512, 256, 128))
    qlat = mm2d(h2, wts['w_q'], epilogue=_epi_rms, tn=wts['w_q'].shape[1], tm=_pick(M, (512, 256, 128)),
                aux=[(wts['q_norm'], (1, wts['w_q'].shape[1]), lambda g, i, j: (0, 0))],
                out_dtype=BF16, name="ab_in_q")
    nsm = wts['w_small'].shape[1]
    small = mm2d(h2, wts['w_small'], epilogue=functools.partial(_epi_ckv, rank=kv_rank), tn=nsm,
                 tm=_pick(M, (512, 256, 128)),
                 aux=[(wts['kv_norm'], (1, kv_rank), lambda g, i, j: (0, 0))], name="ab_in_kv")
    ckv = small[:, :kv_rank]
    kpks = small[:, kv_rank:]

    ones = jnp.ones((tmq, 64), F32)
    zeros = jnp.zeros((tmq, 64), F32)
    if rope:
        cos, sin = _rope_tables(L)
    else:
        cos, sin = ones, zeros
    one128 = jnp.ones((cos.shape[0], LANES), F32)
    tq = jnp.concatenate([one128, cos, sin], axis=-1) * scale
    nq_t = tq.shape[0] // tmq
    q = mm2d(qlat, wts['uq_aug'], epilogue=_epi_table, tm=tmq,
             aux=[(tq, (tmq, 2 * LANES), lambda g, i, j: (i % nq_t, 0))], out_dtype=BF16, name="mla_uq")

    if ctx_ckv is not None:
        past = ctx_ckv.shape[1]
        ckv_all = jnp.concatenate([ctx_ckv.astype(BF16), ckv.reshape(nseq, L, kv_rank).astype(BF16)], axis=1)
        ck = ctx_kpe.astype(F32)
        ctx_kp = jnp.concatenate([ck, ck, jnp.zeros_like(ck), jnp.zeros_like(ck)], axis=-1)
        kpks_all = jnp.concatenate([ctx_kp, kpks.reshape(nseq, L, 2 * LANES)], axis=1)
        tk_ctx = jnp.concatenate([jnp.ones((past, LANES), F32), jnp.zeros((past, LANES), F32)], axis=-1)
        tk_all = jnp.concatenate([tk_ctx, jnp.concatenate([cos, cos, sin, sin], axis=-1)], axis=0)
        Lk = past + L
    else:
        ckv_all = ckv.reshape(nseq, L, kv_rank).astype(BF16)
        kpks_all = kpks.reshape(nseq, L, 2 * LANES)
        Lk = L
        tk_all = None
    tmk = _pick(Lk, (512, 256, 128))
    if tk_all is None:
        tk_all = jnp.concatenate([jnp.ones((tmk, LANES), F32), jnp.zeros((tmk, LANES), F32)], axis=-1)
    nk_t = tk_all.shape[0] // tmk
    Mk = nseq * Lk
    ckv_all = ckv_all.reshape(Mk, kv_rank)
    kcat = mm2d(ckv_all, wts['ukn_aug'], epilogue=_epi_kadd, tm=tmk,
                aux=[(kpks_all.reshape(Mk, 2 * LANES), (tmk, 2 * LANES), lambda g, i, j: (i, 0)),
                     (tk_all, (tmk, 2 * LANES), lambda g, i, j: (i % nk_t, 0))],
                out_dtype=BF16, name="mla_ukn")
    v = mm2d(ckv_all, wts['uv'], tm=tmk, out_dtype=BF16, name="mla_uv")
    att = attention(q.reshape(nseq, L, -1), kcat.reshape(nseq, Lk, -1), v.reshape(nseq, Lk, -1), heads)

    Dc = S5_CHUNK
    R = M // Dc
    n = L // Dc
    y_intra = mm(u2, w_t, name="s5_intra")
    g = mm(u2, w_b, name="s5_to_state")
    s_in, fin = s5_scan(g, ad, h0, n)
    tms = _pick(R, (512, 256, 128))
    tns = Dc * LANES
    gyb = mm(s_in, w_c, epilogue=_epi_add_gelu, tm=tms, tn=tns,
             aux=[(y_intra, (1, tms, tns), lambda g_, i, j: (g_, i, j))], out_mode="unfold_rows", fold=Dc,
             name="s5_from_state")
    s5w = nb * LANES
    tmg = _pick(M, (1024, 512, 256, 128))
    tng = _pick(s5w, (1024, 512, 256, 128))
    s5_out = mm(gyb, glu_w.astype(BF16)[None], epilogue=_epi_glu, a_blocked=True, tm=tmg, tn=tng,
                aux=[(gyb, (tng // LANES, tmg, LANES), lambda g_, i, j: (j, i, 0)),
                     (glu_b[None], (1, tng), lambda g_, i, j: (0, j))], out_dtype=BF16, name="s5_glu")[0]

    cat = jnp.concatenate([s5_out, att.reshape(M, -1)], axis=-1)
    y = mm2d(cat, wts['w_out'], out_dtype=BF16, name="ab_out")
    return y.reshape(nseq, L, D), ckv, kpks[:, :64], fin


def _pack_s5_state(re, im, nb):
    nseq = re.shape[0]
    def blk(x):
        return x.reshape(nseq, nb, -1).transpose(1, 0, 2)
    parts = [blk(re[:, 0]), blk(im[:, 0]), blk(re[:, 1]), blk(im[:, 1])]
    return jnp.concatenate(parts, axis=-1)[:, :, None, :].astype(F32)


def _unpack_s5_state(fin, groups, states):
    nb, nseq = fin.shape[0], fin.shape[1]
    half = fin.shape[-1] // 4
    def blk(x):
        return x.transpose(1, 0, 2).reshape(nseq, groups, states)
    f = fin[:, :, 0]
    re = jnp.stack([blk(f[..., 0:half]), blk(f[..., 2 * half:3 * half])], axis=1)
    im = jnp.stack([blk(f[..., half:2 * half]), blk(f[..., 3 * half:])], axis=1)
    return re, im


def _hg_mixer(h, w_in, w_out, j, lb, out_norm, s0, heads):
    nseq, L, D = h.shape
    M = nseq * L
    z = mm2d(h.reshape(M, D), w_in, wg=j, name="hg_in").reshape(nseq, L, -1)
    o_f, o_b, st = hgrn2(z, lb, s0, heads)
    o = hg_post(o_f, o_b, z, out_norm, heads)
    y = mm2d(o.reshape(M, -1), w_out, wg=j, out_dtype=BF16, name="hg_out")
    return y.reshape(nseq, L, D), st


def _mlp(h, w1, w2, layer):
    nseq, L, D = h.shape
    M = nseq * L
    z = mm2d(h.reshape(M, D), w1, wg=layer, epilogue=_epi_relu2, out_dtype=BF16, name="mlp_up")
    return mm2d(z, w2, wg=layer, out_dtype=BF16, name="mlp_down").reshape(nseq, L, D)


def kernel(x_prompt, x_sample, cache_ckv, cache_kpe, state_s5_re, state_s5_im, state_hgrn, c, c_ctx,
           mod_w, mod_b, norm_g, mlp_w1, mlp_w2, ab_w_in, ab_w_out, mla_q_norm, mla_kv_norm, mla_w_uq,
           mla_w_ukv, s5_log_dt, s5_lam_re, s5_lam_im, s5_b_re, s5_b_im, s5_c_re, s5_c_im, s5_d, s5_glu_w,
           s5_glu_b, hg_w_in, hg_w_out, hg_lower_bounds, hg_out_norm):
    depth = mod_w.shape[0]
    D = x_prompt.shape[-1]
    nsmp = x_sample.shape[0]
    s5w = s5_glu_w.shape[-1]
    q_rank = mla_q_norm.shape[-1]
    kv_rank = mla_kv_norm.shape[-1]
    mla_heads = (D - s5w) // 128
    hg_heads = hg_out_norm.shape[-1] // 128
    groups, states = s5_lam_re.shape[2], s5_lam_re.shape[3]

    lbs = jax.nn.softmax(hg_lower_bounds.astype(F32), axis=1)
    lbs = jnp.cumsum(lbs, axis=1) - lbs[:, :1]

    n_cond = nsmp + 1
    pad = (-n_cond) % 8
    cond = jnp.concatenate([c, c_ctx[None], jnp.zeros((pad, D), F32)], axis=0)

    mods = [modulation(cond, mod_w, mod_b, layer) for layer in range(depth)]
    w1_all, w2_all = mlp_w1.astype(BF16), mlp_w2.astype(BF16)
    hg_in_all, hg_out_all = hg_w_in.astype(BF16), hg_w_out.astype(BF16)

    xp, xs = x_prompt, x_sample
    hp = hs = None
    l_ckv, l_kpe, l_s5r, l_s5i, l_hg = [], [], [], [], []
    for layer in range(depth):
        j = layer // 2
        m = mods[layer]
        ms = [m[:nsmp, i * D:(i + 1) * D][:, None, :] for i in range(6)]
        mp = [m[nsmp:nsmp + 1, i * D:(i + 1) * D][:, None, :] for i in range(6)]
        if layer == 0:
            hp = norm_mod(xp, norm_g[layer, 0], mp[1], mp[0])
            hs = norm_mod(xs, norm_g[layer, 0], ms[1], ms[0])
        if layer % 2 == 0:
            wts = _ab_weights(ab_w_in[j], ab_w_out[j], mla_q_norm[j], mla_kv_norm[j], mla_w_uq[j], mla_w_ukv[j],
                              s5w, q_rank, kv_rank, mla_heads)
            s5m = _s5_weights(s5_log_dt[j], s5_lam_re[j], s5_lam_im[j], s5_b_re[j], s5_b_im[j],
                              s5_c_re[j], s5_c_im[j], s5_d[j])
            nb = s5m[0].shape[0]
            zero_h0 = jnp.zeros((nb, xp.shape[0], 1, s5m[3].shape[-1]), F32)
            yp, ckv, kpe, fin = _ab_mixer(hp, wts, s5m, s5_glu_w[j], s5_glu_b[j], heads=mla_heads,
                                          kv_rank=kv_rank, ctx_ckv=None, ctx_kpe=None, h0=zero_h0, rope=False)
            h0s = _pack_s5_state(state_s5_re[:, j], state_s5_im[:, j], nb)
            ys, _, _, _ = _ab_mixer(hs, wts, s5m, s5_glu_w[j], s5_glu_b[j], heads=mla_heads, kv_rank=kv_rank,
                                    ctx_ckv=cache_ckv[:, j], ctx_kpe=cache_kpe[:, j], h0=h0s, rope=True)
            hr, hi = _unpack_s5_state(fin, groups, states)
            l_ckv.append(ckv.reshape(xp.shape[0], xp.shape[1], kv_rank))
            l_kpe.append(kpe.reshape(xp.shape[0], xp.shape[1], 64))
            l_s5r.append(hr)
            l_s5i.append(hi)
        else:
            yp, st = _hg_mixer(hp, hg_in_all, hg_out_all, j, lbs[:, layer], hg_out_norm[j], None, hg_heads)
            ys, _ = _hg_mixer(hs, hg_in_all, hg_out_all, j, lbs[:, layer], hg_out_norm[j], state_hgrn[:, j],
                              hg_heads)
            l_hg.append(st)
        xp, hp = resid_norm(xp, yp, norm_g[layer, 1], mp[2], (norm_g[layer, 2], mp[4], mp[3]))
        xs, hs = resid_norm(xs, ys, norm_g[layer, 1], ms[2], (norm_g[layer, 2], ms[4], ms[3]))
        yp = _mlp(hp, w1_all, w2_all, layer)
        ys = _mlp(hs, w1_all, w2_all, layer)
        if layer + 1 < depth:
            m_n = mods[layer + 1]
            nxt_s = (norm_g[layer + 1, 0], m_n[:nsmp, D:2 * D][:, None, :], m_n[:nsmp, 0:D][:, None, :])
            nxt_p = (norm_g[layer + 1, 0], m_n[nsmp:nsmp + 1, D:2 * D][:, None, :],
                     m_n[nsmp:nsmp + 1, 0:D][:, None, :])
        else:
            nxt_s = nxt_p = None
        xp, hp = resid_norm(xp, yp, norm_g[layer, 3], mp[5], nxt_p)
        xs, hs = resid_norm(xs, ys, norm_g[layer, 3], ms[5], nxt_s)
    new_ckv = jnp.stack(l_ckv, axis=1)
    new_kpe = jnp.stack(l_kpe, axis=1)
    new_s5_re = jnp.stack(l_s5r, axis=1)
    new_s5_im = jnp.stack(l_s5i, axis=1)
    new_hgrn = jnp.stack(l_hg, axis=1)
    return (xp, xs, new_ckv, new_kpe, new_s5_re, new_s5_im, new_hgrn)
```

```python
import functools
import math

import jax
import jax.numpy as jnp
from jax import lax
from jax.experimental import pallas as pl
from jax.experimental.pallas import tpu as pltpu

F32 = jnp.float32
BF16 = jnp.bfloat16

EPS = 1e-6
GRID_W = 64
ROPE_BASE = 10000.0
LANES = 128
S5_CHUNK = 16
HG_SUB = 32
HG_BLOCK = 2 * HG_SUB
VMEM_LIMIT = 56 * 1024 * 1024
MM_VMEM_BUDGET = 42 * 1024 * 1024
ATT_HEADS = 2
HG_POST_HEADS = 8
HG_HEADS = 4
HG_BLOCKS = 16
S5_SEQS = 4


def _pick(n, prefs):
    for p in prefs:
        if n % p == 0:
            return p
    return n


def _cparams(sem):
    return pltpu.CompilerParams(dimension_semantics=sem, vmem_limit_bytes=VMEM_LIMIT)


def _sigmoid(x):
    return 0.5 + 0.5 * jnp.tanh(0.5 * x)


def _silu(x):
    return x * _sigmoid(x)


def _gelu_tanh(x):
    c = math.sqrt(2.0 / math.pi)
    return 0.5 * x * (1.0 + jnp.tanh(c * (x + 0.044715 * (x * x * x))))


def _rms(x):
    return x * lax.rsqrt(jnp.mean(x * x, axis=-1, keepdims=True) + EPS)


def _mm_kernel(*refs, nk, n_aux, epilogue, a_blocked, out_mode, fold):
    a_ref, w_ref = refs[0], refs[1]
    aux_refs = refs[2:2 + n_aux]
    o_ref = refs[2 + n_aux]
    scratch = list(refs[3 + n_aux:])

    def load_a():
        if a_blocked:
            return jnp.concatenate([a_ref[c] for c in range(a_ref.shape[0])], axis=-1).astype(BF16)
        return a_ref[0].astype(BF16)

    def finish(acc):
        aux = []
        for r in aux_refs:
            v = r[...]
            aux.append(v)
        out = epilogue(acc, *aux) if epilogue is not None else acc
        if out_mode == "fold_rows":
            slab_ref = scratch.pop()
            rows = slab_ref.shape[1] // fold
            for c in range(o_ref.shape[0]):
                slab_ref[c] = out[:, c * LANES:(c + 1) * LANES]
            for c in range(o_ref.shape[0]):
                for t in range(fold):
                    o_ref[c, :, t * LANES:(t + 1) * LANES] = (
                        slab_ref[c, pl.ds(t, rows, stride=fold), :].astype(o_ref.dtype))
        elif out_mode == "unfold_rows":
            rows = out.shape[0]
            for t in range(fold):
                o_ref[0, pl.ds(t, rows, stride=fold), :] = out[:, t * LANES:(t + 1) * LANES].astype(o_ref.dtype)
        else:
            o_ref[0] = out.astype(o_ref.dtype)

    if nk == 1:
        finish(jnp.dot(load_a(), w_ref[0].astype(BF16), preferred_element_type=F32))
        return

    acc_ref = scratch.pop(0)
    k = pl.program_id(3)

    @pl.when(k == 0)
    def _():
        acc_ref[...] = jnp.zeros_like(acc_ref)

    acc_ref[...] += jnp.dot(load_a(), w_ref[0].astype(BF16), preferred_element_type=F32)

    @pl.when(k == nk - 1)
    def _():
        finish(acc_ref[...])


def _pick_tk(K, tm, tn, a_bytes, w_bytes, out_bytes, aux_bytes):
    for tk in (K, 4096, 2048, 1024, 512, 256, 128):
        if tk > K or K % tk:
            continue
        acc = 0 if tk == K else 4 * tm * tn
        est = 2 * (tm * tk * a_bytes + tk * tn * w_bytes) + 2 * tm * tn * out_bytes + acc + 2 * aux_bytes
        if est <= MM_VMEM_BUDGET:
            return tk
    return LANES


def mm(a, w, *, wg=0, epilogue=None, aux=(), out_dtype=F32, tm=None, tn=None, tk=None,
       a_blocked=False, out_mode="plain", fold=1, name="mm"):
    _, K, N = w.shape
    G = 1 if a_blocked else a.shape[0]
    M = a.shape[1]
    tm = tm or _pick(M, (1024, 512, 256, 128))
    tn = tn or _pick(N, (1024, 512, 256, 128))
    if tk is None:
        aux_bytes = sum(math.prod(bshape) * arr.dtype.itemsize for arr, bshape, _ in aux)
        tk = _pick_tk(K, tm, tn, a.dtype.itemsize, w.dtype.itemsize, jnp.dtype(out_dtype).itemsize, aux_bytes)
    assert M % tm == 0 and N % tn == 0 and K % tk == 0, (M, N, K, tm, tn, tk)
    nk = K // tk
    grid = (G, M // tm, N // tn, nk)
    if a_blocked:
        a_spec = pl.BlockSpec((tk // LANES, tm, LANES), lambda g, i, j, k: (k, i, 0))
    else:
        a_spec = pl.BlockSpec((1, tm, tk), lambda g, i, j, k: (g, i, k))
    w_spec = pl.BlockSpec((1, tk, tn), lambda g, i, j, k: (wg + g, k, j))
    aux_arrays, aux_specs = [], []
    for arr, bshape, imap in aux:
        aux_arrays.append(arr)
        aux_specs.append(pl.BlockSpec(bshape, functools.partial(lambda g, i, j, k, f: f(g, i, j), f=imap)))
    scratch = [] if nk == 1 else [pltpu.VMEM((tm, tn), F32)]
    if out_mode == "fold_rows":
        assert G == 1 and tm % fold == 0
        out_shape = jax.ShapeDtypeStruct((N // LANES, M // fold, fold * LANES), out_dtype)
        out_spec = pl.BlockSpec((tn // LANES, tm // fold, fold * LANES), lambda g, i, j, k: (j, i, 0))
        scratch.append(pltpu.VMEM((tn // LANES, tm, LANES), F32))
    elif out_mode == "unfold_rows":
        assert tn == N == fold * LANES
        out_shape = jax.ShapeDtypeStruct((G, M * fold, LANES), out_dtype)
        out_spec = pl.BlockSpec((1, tm * fold, LANES), lambda g, i, j, k: (g, i, 0))
    else:
        out_shape = jax.ShapeDtypeStruct((G, M, N), out_dtype)
        out_spec = pl.BlockSpec((1, tm, tn), lambda g, i, j, k: (g, i, j))
    kern = functools.partial(_mm_kernel, nk=nk, n_aux=len(aux_arrays), epilogue=epilogue,
                             a_blocked=a_blocked, out_mode=out_mode, fold=fold)
    return pl.pallas_call(
        kern, grid=grid, in_specs=[a_spec, w_spec] + aux_specs, out_specs=out_spec,
        out_shape=out_shape, scratch_shapes=scratch, name=name,
        compiler_params=_cparams(("parallel", "parallel", "parallel", "arbitrary")),
    )(a, w, *aux_arrays)


def mm2d(a, w, **kw):
    out = mm(a[None], w if w.ndim == 3 else w[None], **kw)
    return out if kw.get("out_mode", "plain") != "plain" else out[0]


def _epi_relu2(acc):
    r = jnp.maximum(acc, 0.0)
    return r * r


def _epi_rms(acc, g):
    return _rms(acc) * g


def _epi_ckv(acc, g, *, rank):
    return jnp.concatenate([_rms(acc[:, :rank]) * g, acc[:, rank:]], axis=-1)


def _epi_table(acc, t):
    reps = acc.shape[1] // t.shape[1]
    return acc * jnp.concatenate([t] * reps, axis=-1)


def _epi_kadd(acc, kpks, t):
    kr2 = kpks[:, :LANES] * t[:, :LANES] + kpks[:, LANES:] * t[:, LANES:]
    blk = jnp.concatenate([jnp.zeros_like(kr2), kr2], axis=-1)
    reps = acc.shape[1] // blk.shape[1]
    return acc + jnp.concatenate([blk] * reps, axis=-1)


def _epi_add_gelu(acc, y0):
    return _gelu_tanh(acc + y0[0])


def _epi_glu(acc, gy, b):
    y = jnp.concatenate([gy[c] for c in range(gy.shape[0])], axis=-1).astype(F32)
    return y * _sigmoid(acc + b)


def _mod_kernel(c_ref, w_ref, b_ref, o_ref, *, nk):
    k = pl.program_id(1)

    @pl.when(k == 0)
    def _():
        o_ref[...] = jnp.zeros_like(o_ref)

    a = _silu(c_ref[...]).astype(BF16)
    o_ref[...] += jnp.dot(a, w_ref[...].astype(BF16), preferred_element_type=F32)

    @pl.when(k == nk - 1)
    def _():
        o_ref[...] += b_ref[...]


def modulation(cond, w, b, layer):
    rows, d = cond.shape
    n = w.shape[2]
    tn = _pick(n, (2048, 1024, 512, 256, 128))
    tk = _pick(d, (1024, 512, 256, 128))
    nk = d // tk
    return pl.pallas_call(
        functools.partial(_mod_kernel, nk=nk), grid=(n // tn, nk),
        in_specs=[pl.BlockSpec((rows, tk), lambda j, k: (0, k)),
                  pl.BlockSpec((None, tk, tn), lambda j, k: (layer, k, j)),
                  pl.BlockSpec((None, 1, tn), lambda j, k: (layer, 0, j))],
        out_specs=pl.BlockSpec((rows, tn), lambda j, k: (0, j)),
        out_shape=jax.ShapeDtypeStruct((rows, n), F32), name="modulation",
        compiler_params=_cparams(("parallel", "arbitrary")),
    )(cond, w, b[:, None, :])


def _norm_mod_kernel(x_ref, g_ref, sc_ref, sh_ref, o_ref):
    y = _rms(x_ref[0]) * g_ref[...]
    o_ref[0] = (y * (1.0 + sc_ref[0]) + sh_ref[0]).astype(o_ref.dtype)


def norm_mod(x, g, sc, sh):
    nseq, L, D = x.shape
    tr = _pick(L, (256, 128))
    per_seq = sc.shape[0] == nseq and nseq > 1
    smap = (lambda b, i: (b, 0, 0)) if per_seq else (lambda b, i: (0, 0, 0))
    return pl.pallas_call(
        _norm_mod_kernel, grid=(nseq, L // tr),
        in_specs=[pl.BlockSpec((1, tr, D), lambda b, i: (b, i, 0)),
                  pl.BlockSpec((1, D), lambda b, i: (0, 0)),
                  pl.BlockSpec((1, 1, D), smap), pl.BlockSpec((1, 1, D), smap)],
        out_specs=pl.BlockSpec((1, tr, D), lambda b, i: (b, i, 0)),
        out_shape=jax.ShapeDtypeStruct((nseq, L, D), BF16), name="norm_mod",
        compiler_params=_cparams(("parallel", "parallel")),
    )(x, g[None], sc, sh)


def _resid_kernel(x_ref, y_ref, g1_ref, gt_ref, *rest, with_h):
    xn = x_ref[0] + gt_ref[0] * (_rms(y_ref[0].astype(F32)) * g1_ref[...])
    if with_h:
        g2_ref, sc_ref, sh_ref, xo_ref, h_ref = rest
        xo_ref[0] = xn
        h_ref[0] = ((_rms(xn) * g2_ref[...]) * (1.0 + sc_ref[0]) + sh_ref[0]).astype(h_ref.dtype)
    else:
        (xo_ref,) = rest
        xo_ref[0] = xn


def resid_norm(x, y, g1, gate, nxt=None):
    nseq, L, D = x.shape
    tr = _pick(L, (256, 128))
    per_seq = gate.shape[0] == nseq and nseq > 1
    smap = (lambda b, i: (b, 0, 0)) if per_seq else (lambda b, i: (0, 0, 0))
    row = pl.BlockSpec((1, tr, D), lambda b, i: (b, i, 0))
    vec = pl.BlockSpec((1, D), lambda b, i: (0, 0))
    mod = pl.BlockSpec((1, 1, D), smap)
    args = [x, y, g1[None], gate]
    specs = [row, row, vec, mod]
    out_shape = [jax.ShapeDtypeStruct((nseq, L, D), F32)]
    out_specs = [row]
    if nxt is not None:
        g2, sc, sh = nxt
        args += [g2[None], sc, sh]
        specs += [vec, mod, mod]
        out_shape.append(jax.ShapeDtypeStruct((nseq, L, D), BF16))
        out_specs.append(row)
    res = pl.pallas_call(
        functools.partial(_resid_kernel, with_h=nxt is not None), grid=(nseq, L // tr),
        in_specs=specs, out_specs=out_specs, out_shape=out_shape, name="resid_norm",
        compiler_params=_cparams(("parallel", "parallel")),
    )(*args)
    return (res[0], res[1]) if nxt is not None else (res[0], None)


def _attn_kernel(q_ref, k_ref, v_ref, o_ref, *, hp):
    scores = []
    for h in range(hp):
        qk = slice(h * 2 * LANES, (h + 1) * 2 * LANES)
        scores.append(lax.dot_general(q_ref[0, :, qk], k_ref[0, :, qk], (((1,), (1,)), ((), ())),
                                      preferred_element_type=F32))
    probs = []
    for s in scores:
        m = jnp.max(s, axis=-1, keepdims=True)
        p = jnp.exp2(s - m)
        probs.append((p.astype(BF16), jnp.sum(p, axis=-1, keepdims=True)))
    for h, (p, l) in enumerate(probs):
        vo = slice(h * LANES, (h + 1) * LANES)
        o = jnp.dot(p, v_ref[0, :, vo], preferred_element_type=F32)
        o_ref[0, :, vo] = (o / l).astype(o_ref.dtype)


def attention(q, kcat, v, heads):
    nseq, L, _ = q.shape
    Lk = kcat.shape[1]
    tq = _pick(L, (256, 128))
    hp = ATT_HEADS if heads % ATT_HEADS == 0 else 1
    return pl.pallas_call(
        functools.partial(_attn_kernel, hp=hp), grid=(nseq, heads // hp, L // tq),
        in_specs=[pl.BlockSpec((1, tq, hp * 2 * LANES), lambda b, h, i: (b, i, h)),
                  pl.BlockSpec((1, Lk, hp * 2 * LANES), lambda b, h, i: (b, 0, h)),
                  pl.BlockSpec((1, Lk, hp * LANES), lambda b, h, i: (b, 0, h))],
        out_specs=pl.BlockSpec((1, tq, hp * LANES), lambda b, h, i: (b, i, h)),
        out_shape=jax.ShapeDtypeStruct((nseq, L, heads * LANES), BF16), name="mla_attention",
        compiler_params=_cparams(("parallel", "parallel", "arbitrary")),
    )(q, kcat, v)


def _s5_scan_kernel(g_ref, ad_ref, h0_ref, sin_ref, fin_ref, *, n, sb, half):
    ad = ad_ref[0]
    afr, afi = ad[:, 0:half], ad[:, half:2 * half]
    abr, abi = ad[:, 2 * half:3 * half], ad[:, 3 * half:4 * half]

    def body(k, carry):
        out = []
        for s in range(sb):
            fr, fi, br, bi = carry[4 * s:4 * s + 4]
            row_f = pl.ds(s * n + k, 1)
            row_b = pl.ds(s * n + n - 1 - k, 1)
            sin_ref[0, row_f, 0:2 * half] = jnp.concatenate([fr, fi], axis=-1)
            sin_ref[0, row_b, 2 * half:4 * half] = jnp.concatenate([br, bi], axis=-1)
            gf = g_ref[0, row_f, 0:2 * half]
            gb = g_ref[0, row_b, 2 * half:4 * half]
            out += [afr * fr - afi * fi + gf[:, :half], afr * fi + afi * fr + gf[:, half:],
                    abr * br - abi * bi + gb[:, :half], abr * bi + abi * br + gb[:, half:]]
        return tuple(out)

    init = []
    for s in range(sb):
        h0 = h0_ref[0, s]
        init += [h0[:, 0:half], h0[:, half:2 * half], h0[:, 2 * half:3 * half], h0[:, 3 * half:4 * half]]
    fin = lax.fori_loop(0, n, body, tuple(init))
    for s in range(sb):
        fin_ref[0, s] = jnp.concatenate(fin[4 * s:4 * s + 4], axis=-1)


def s5_scan(g, ad, h0, n):
    nb, rows, W = g.shape
    nseq = rows // n
    sb = S5_SEQS if nseq % S5_SEQS == 0 else 1
    return pl.pallas_call(
        functools.partial(_s5_scan_kernel, n=n, sb=sb, half=W // 4), grid=(nb, nseq // sb),
        in_specs=[pl.BlockSpec((1, sb * n, W), lambda j, b: (j, b, 0)),
                  pl.BlockSpec((1, 1, W), lambda j, b: (j, 0, 0)),
                  pl.BlockSpec((1, sb, 1, W), lambda j, b: (j, b, 0, 0))],
        out_specs=[pl.BlockSpec((1, sb * n, W), lambda j, b: (j, b, 0)),
                   pl.BlockSpec((1, sb, 1, W), lambda j, b: (j, b, 0, 0))],
        out_shape=[jax.ShapeDtypeStruct((nb, rows, W), F32),
                   jax.ShapeDtypeStruct((nb, nseq, 1, W), F32)], name="s5_scan",
        compiler_params=_cparams(("parallel", "parallel")),
    )(g, ad, h0)


def _hg_masks(rev):
    T = HG_BLOCK
    row = lax.broadcasted_iota(jnp.int32, (T, LANES), 0)
    first = row < HG_SUB
    r2 = lax.broadcasted_iota(jnp.int32, (T, 2 * T), 0)
    c2 = lax.broadcasted_iota(jnp.int32, (T, 2 * T), 1)
    s2 = c2 & (T - 1)
    sub_bits = HG_SUB.bit_length() - 1
    other_sub = (r2 ^ s2) >> sub_bits
    causal = (s2 >= r2) if rev else (s2 <= r2)
    keep = jnp.where(other_sub == (c2 >> (sub_bits + 1)), jnp.where(causal, 1, 0), 0) > 0
    rin = row & (HG_SUB - 1)
    steps = (1, 2, 4, 8, 16)
    scan = [(rin < HG_SUB - s) if rev else (rin >= s) for s in steps]
    return steps, scan, first, keep


def _hg_direction(q, z, v, lb, st, rev, masks):
    T = HG_BLOCK
    steps, scan, first, keep = masks
    qa = _silu(q) * (LANES ** -0.5)
    th = 0.5 * jnp.tanh(0.5 * z)
    sig = 0.5 + th
    nsig = 0.5 - th
    kk = (1.0 - lb) * nsig
    g = jnp.log(lb + (1.0 - lb) * sig)

    b = g
    for s, ok in zip(steps, scan):
        b = b + jnp.where(ok, pltpu.roll(b, (T - s) if rev else s, axis=0), 0.0)
    if not rev:
        b0, b1 = b[HG_SUB - 1:HG_SUB], b[T - 1:T]
    else:
        b0, b1 = b[0:1], b[HG_SUB:HG_SUB + 1]
    bsub = jnp.where(first, b0, b1)
    qh = qa * jnp.exp(b)
    kd = kk * jnp.exp(-b)
    ke = kk * jnp.exp(bsub - b)

    kcat = jnp.concatenate([kd, ke], axis=0).astype(BF16)
    att = lax.dot_general(qh.astype(BF16), kcat, (((1,), (1,)), ((), ())), preferred_element_type=F32)
    att = jnp.where(keep, att, 0.0)
    att = att + pltpu.roll(att, T, axis=1)
    vb = v.astype(BF16)
    o = jnp.dot(att[:, :T].astype(BF16), vb, preferred_element_type=F32)

    if not rev:
        dq = jnp.where(first, 1.0, jnp.exp(b0))
        ek = jnp.where(first, jnp.exp(b1), 1.0)
    else:
        dq = jnp.where(first, jnp.exp(b1), 1.0)
        ek = jnp.where(first, 1.0, jnp.exp(b0))
    o = o + lax.dot_general((qh * dq).astype(BF16), st.astype(BF16), (((1,), (1,)), ((), ())),
                            preferred_element_type=F32)
    upd = lax.dot_general(vb, (ke * ek).astype(BF16), (((0,), (0,)), ((), ())), preferred_element_type=F32)
    st_new = st * jnp.exp(b0 + b1) + upd
    return o, st_new


def _hg_kernel(*refs, nsteps, nb, hp, has_init):
    if has_init:
        (qf_ref, zf_ref, vf_ref, qb_ref, zb_ref, vb_ref, lb_ref, s0_ref,
         of_ref, ob_ref, so_ref, stf_ref, stb_ref) = refs
    else:
        (qf_ref, zf_ref, vf_ref, qb_ref, zb_ref, vb_ref, lb_ref,
         of_ref, ob_ref, so_ref, stf_ref, stb_ref) = refs
    i = pl.program_id(2)
    T = HG_BLOCK

    @pl.when(i == 0)
    def _():
        for h in range(hp):
            if has_init:
                stf_ref[h] = s0_ref[0, 0, h].T
                stb_ref[h] = s0_ref[0, 1, h].T
            else:
                stf_ref[h] = jnp.zeros((LANES, LANES), F32)
                stb_ref[h] = jnp.zeros((LANES, LANES), F32)

    lb = lb_ref[...]
    masks_f = _hg_masks(False)
    masks_b = _hg_masks(True)
    for h in range(hp):
        lanes = slice(h * LANES, (h + 1) * LANES)
        st_f = stf_ref[h]
        st_b = stb_ref[h]
        for blk in range(nb):
            rows = slice(blk * T, (blk + 1) * T)
            o_f, st_f = _hg_direction(qf_ref[0, rows, lanes], zf_ref[0, rows, lanes], vf_ref[0, rows, lanes],
                                      lb[0:1, lanes], st_f, False, masks_f)
            of_ref[0, rows, lanes] = o_f.astype(of_ref.dtype)
        for blk in reversed(range(nb)):
            rows = slice(blk * T, (blk + 1) * T)
            o_b, st_b = _hg_direction(qb_ref[0, rows, lanes], zb_ref[0, rows, lanes], vb_ref[0, rows, lanes],
                                      lb[1:2, lanes], st_b, True, masks_b)
            ob_ref[0, rows, lanes] = o_b.astype(ob_ref.dtype)
        stf_ref[h] = st_f
        stb_ref[h] = st_b

        @pl.when(i == nsteps - 1)
        def _(h=h, st_f=st_f, st_b=st_b):
            so_ref[0, 0, h] = st_f.T
            so_ref[0, 1, h] = st_b.T


def hgrn2(z, lb, s0, heads):
    nseq, L, _ = z.shape
    nb = max(n for n in range(1, HG_BLOCKS + 1) if L % (n * HG_BLOCK) == 0)
    hp = max(h for h in range(1, heads + 1) if heads % h == 0 and h * nb <= max(HG_HEADS * HG_BLOCKS, nb))
    T = nb * HG_BLOCK
    nsteps = L // T
    HB = heads // hp

    def col(off, rev):
        if rev:
            return lambda b, h, i: (b, nsteps - 1 - i, off + h)
        return lambda b, h, i: (b, i, off + h)

    tile = lambda off, rev: pl.BlockSpec((1, T, hp * LANES), col(off, rev))
    in_specs = [tile(0, False), tile(HB, False), tile(3 * HB, False),
                tile(0, True), tile(2 * HB, True), tile(3 * HB, True),
                pl.BlockSpec((2, hp * LANES), lambda b, h, i: (0, h))]
    args = [z, z, z, z, z, z, lb]
    st_spec = pl.BlockSpec((1, 2, hp, LANES, LANES), lambda b, h, i: (b, 0, h, 0, 0))
    if s0 is not None:
        in_specs.append(st_spec)
        args.append(s0)
    o_shape = jax.ShapeDtypeStruct((nseq, L, heads * LANES), BF16)
    st_scratch = pltpu.VMEM((hp, LANES, LANES), F32)
    return pl.pallas_call(
        functools.partial(_hg_kernel, nsteps=nsteps, nb=nb, hp=hp, has_init=s0 is not None),
        grid=(nseq, HB, nsteps), in_specs=in_specs,
        out_specs=[tile(0, False), tile(0, True), st_spec],
        out_shape=[o_shape, o_shape, jax.ShapeDtypeStruct((nseq, 2, heads, LANES, LANES), F32)],
        scratch_shapes=[st_scratch, st_scratch], name="hgrn2",
        compiler_params=_cparams(("parallel", "parallel", "arbitrary")),
    )(*args)


def _hg_post_kernel(of_ref, ob_ref, g_ref, gn_ref, o_ref, *, hp):
    for h in range(hp):
        lanes = slice(h * LANES, (h + 1) * LANES)
        o = of_ref[0, :, lanes].astype(F32) + ob_ref[0, :, lanes].astype(F32)
        o_ref[0, :, lanes] = ((_rms(o) * gn_ref[:, lanes]) * _silu(g_ref[0, :, lanes])).astype(o_ref.dtype)


def hg_post(o_f, o_b, z, out_norm, heads):
    nseq, L, _ = o_f.shape
    tr = _pick(L, (256, 128))
    hp = HG_POST_HEADS if heads % HG_POST_HEADS == 0 else 1
    tile = pl.BlockSpec((1, tr, hp * LANES), lambda b, i, h: (b, i, h))
    return pl.pallas_call(
        functools.partial(_hg_post_kernel, hp=hp), grid=(nseq, L // tr, heads // hp),
        in_specs=[tile, tile, pl.BlockSpec((1, tr, hp * LANES), lambda b, i, h: (b, i, 4 * (heads // hp) + h)),
                  pl.BlockSpec((1, hp * LANES), lambda b, i, h: (0, h))],
        out_specs=tile, out_shape=jax.ShapeDtypeStruct(o_f.shape, BF16), name="hg_post",
        compiler_params=_cparams(("parallel", "parallel", "parallel")),
    )(o_f, o_b, z, out_norm[None])


def _rope_tables(n_l):
    rope = 64
    half = rope // 2
    rows = n_l // GRID_W
    row = jnp.repeat(jnp.arange(rows), GRID_W).astype(F32)
    col = jnp.tile(jnp.arange(GRID_W), rows).astype(F32)
    inv = ROPE_BASE ** (-jnp.arange(0, half, 2, dtype=F32) / half)
    ar = row[:, None] * inv[None]
    ac = col[:, None] * inv[None]
    cos = jnp.concatenate([jnp.cos(ar), jnp.cos(ar), jnp.cos(ac), jnp.cos(ac)], axis=-1)
    sin = jnp.concatenate([jnp.sin(ar), jnp.sin(ar), jnp.sin(ac), jnp.sin(ac)], axis=-1)
    return cos, sin


def _rot_cols(w):
    return jnp.concatenate([-w[..., 16:32], w[..., 0:16], -w[..., 48:64], w[..., 32:48]], axis=-1)


def _s5_weights(log_dt, lam_re, lam_im, b_re, b_im, c_re, c_im, d_skip):
    D = S5_CHUNK
    G, P = lam_re.shape[1], lam_re.shape[2]
    C = b_re.shape[-1]
    gl = LANES // C
    nb = G // gl
    dt = jnp.exp(log_dt.astype(F32))[..., None]
    lr, li = lam_re.astype(F32), lam_im.astype(F32)
    mag = jnp.exp(lr * dt)
    ar, ai = mag * jnp.cos(li * dt), mag * jnp.sin(li * dt)
    den = lr * lr + li * li
    cr_ = ((ar - 1.0) * lr + ai * li) / den
    ci_ = (ai * lr - (ar - 1.0) * li) / den
    br, bi = b_re.astype(F32), b_im.astype(F32)
    bbr = cr_[..., None] * br - ci_[..., None] * bi
    bbi = cr_[..., None] * bi + ci_[..., None] * br
    cr, ci = c_re.astype(F32), c_im.astype(F32)
    def powers(n):
        n = n.astype(F32)[:, None, None, None]
        pmag = jnp.exp(n * (lr * dt)[None])
        return pmag * jnp.cos(n * (li * dt)[None]), pmag * jnp.sin(n * (li * dt)[None])

    def times_b(p_r, p_i):
        return (p_r[..., None] * bbr[None] - p_i[..., None] * bbi[None],
                p_r[..., None] * bbi[None] + p_i[..., None] * bbr[None])

    steps = jnp.arange(D)
    pr, pi = powers(jnp.arange(D + 1))
    abr, abi = times_b(pr[:D], pi[:D])
    abr_dn, abi_dn = times_b(*powers(D - 1 - steps))
    pr_dn, pi_dn = powers(D - steps)
    half = gl * P
    lane_group = jnp.arange(LANES) // C
    tok_lane_group = jnp.tile(lane_group, D)
    state_group = jnp.arange(half) // P

    def response(x, a_r, a_i):
        return jnp.einsum('gcp,ngpk->ngkc', cr[x], a_r) - jnp.einsum('gcp,ngpk->ngkc', ci[x], a_i)

    kf = response(0, abr[:, 0], abi[:, 0])
    kb = response(1, abr_dn[:, 1], abi_dn[:, 1])
    skip = d_skip.astype(F32)[:, :, None] * jnp.eye(C, dtype=F32)[None]
    zpad = jnp.zeros((D - 1,) + kf.shape[1:], F32)
    lagk = (jnp.concatenate([zpad, kf], axis=0) + jnp.concatenate([kb, zpad], axis=0)
            + jnp.concatenate([zpad, skip[None], zpad], axis=0))
    bd = jnp.tile(lagk.reshape(2 * D - 1, nb, LANES, C), (1, 1, 1, gl))
    bd = jnp.where(lane_group[:, None] == lane_group[None, :], bd, 0.0).astype(BF16)
    lag_idx = jnp.arange(D)[None, :] - jnp.arange(D)[:, None] + (D - 1)
    w_t = bd[lag_idx].transpose(2, 0, 3, 1, 4).reshape(nb, D * LANES, D * LANES)

    def to_state(x):
        rows = x.reshape(D, nb, gl, P, C).transpose(1, 0, 2, 4, 3).reshape(nb, D * LANES, P)
        tiled = jnp.tile(rows, (1, 1, gl))
        return jnp.where(tok_lane_group[:, None] == state_group[None, :], tiled, 0.0).astype(BF16)

    w_b = jnp.concatenate([to_state(abr_dn[:, 0]), to_state(abi_dn[:, 0]),
                           to_state(abr[:, 1]), to_state(abi[:, 1])], axis=-1)

    def from_state(y):
        cols = y.reshape(D, nb, gl, C, P).transpose(1, 4, 0, 2, 3).reshape(nb, P, D * LANES)
        tiled = jnp.tile(cols, (1, gl, 1))
        return jnp.where(state_group[:, None] == tok_lane_group[None, :], tiled, 0.0).astype(BF16)

    def c_times(x, p_r, p_i):
        re = cr[x][None] * p_r[:, :, None, :] - ci[x][None] * p_i[:, :, None, :]
        im = cr[x][None] * p_i[:, :, None, :] + ci[x][None] * p_r[:, :, None, :]
        return re, -im

    cf = c_times(0, pr[1:D + 1, 0], pi[1:D + 1, 0])
    cb = c_times(1, pr_dn[:, 1], pi_dn[:, 1])
    w_c = jnp.concatenate([from_state(cf[0]), from_state(cf[1]), from_state(cb[0]), from_state(cb[1])], axis=1)
    adr = pr[D].reshape(2, nb, half)
    adi = pi[D].reshape(2, nb, half)
    ad = jnp.concatenate([adr[0], adi[0], adr[1], adi[1]], axis=-1)[:, None, :]
    return w_t, w_b, w_c, ad


def _ab_weights(w_in, w_out, q_norm, kv_norm, w_uq, w_ukv, s5w, q_rank, kv_rank, heads):
    rope, nope, vdim = 64, 128, 128
    o1, o2, o3 = s5w, s5w + q_rank, s5w + q_rank + kv_rank
    w_u = w_in[:, :o1].astype(BF16)
    w_q = w_in[:, o1:o2].astype(BF16)
    w_kp = w_in[:, o3:]
    w_ks = _rot_cols(w_kp)
    w_small = jnp.concatenate([w_in[:, o2:o3], w_kp, w_kp, w_ks, w_ks], axis=-1).astype(BF16)
    uq = w_uq.reshape(q_rank, heads, nope + rope)
    uq_aug = jnp.concatenate([uq[..., :nope], uq[..., nope:], _rot_cols(uq[..., nope:])], axis=-1)
    uq_aug = uq_aug.reshape(q_rank, heads * 2 * LANES).astype(BF16)
    ukv = w_ukv.reshape(kv_rank, heads, nope + vdim)
    ukn_aug = jnp.concatenate([ukv[..., :nope], jnp.zeros_like(ukv[..., :nope])], axis=-1)
    ukn_aug = ukn_aug.reshape(kv_rank, heads * 2 * LANES).astype(BF16)
    uv = ukv[..., nope:].reshape(kv_rank, heads * vdim).astype(BF16)
    return dict(w_u=w_u, w_q=w_q, w_small=w_small, uq_aug=uq_aug, ukn_aug=ukn_aug, uv=uv,
                w_out=w_out.astype(BF16), q_norm=q_norm[None], kv_norm=kv_norm[None])


def _ab_mixer(h, wts, s5m, glu_w, glu_b, *, heads, kv_rank, ctx_ckv, ctx_kpe, h0, rope):
    nseq, L, D = h.shape
    M = nseq * L
    h2 = h.reshape(M, D)
    w_t, w_b, w_c, ad = s5m
    nb = w_t.shape[0]
    scale = (128 + 64) ** -0.5 * math.log2(math.e)

    u2 = mm2d(h2, wts['w_u'], out_dtype=BF16, out_mode="fold_rows", fold=S5_CHUNK, name="ab_in_u")
    tmq = _pick(L, (1024, 512, 256, 128))
    qlat = mm2d(h2, wts['w_q'], epilogue=_epi_rms, tn=wts['w_q'].shape[1], tm=_pick(M, (512, 256, 128)),
                aux=[(wts['q_norm'], (1, wts['w_q'].shape[1]), lambda g, i, j: (0, 0))],
                out_dtype=BF16, name="ab_in_q")
    nsm = wts['w_small'].shape[1]
    small = mm2d(h2, wts['w_small'], epilogue=functools.partial(_epi_ckv, rank=kv_rank), tn=nsm,
                 tm=_pick(M, (512, 256, 128)),
                 aux=[(wts['kv_norm'], (1, kv_rank), lambda g, i, j: (0, 0))], name="ab_in_kv")
    ckv = small[:, :kv_rank]
    kpks = small[:, kv_rank:]

    ones = jnp.ones((tmq, 64), F32)
    zeros = jnp.zeros((tmq, 64), F32)
    if rope:
        cos, sin = _rope_tables(L)
    else:
        cos, sin = ones, zeros
    one128 = jnp.ones((cos.shape[0], LANES), F32)
    tq = jnp.concatenate([one128, cos, sin], axis=-1) * scale
    nq_t = tq.shape[0] // tmq
    q = mm2d(qlat, wts['uq_aug'], epilogue=_epi_table, tm=tmq,
             aux=[(tq, (tmq, 2 * LANES), lambda g, i, j: (i % nq_t, 0))], out_dtype=BF16, name="mla_uq")

    if ctx_ckv is not None:
        past = ctx_ckv.shape[1]
        ckv_all = jnp.concatenate([ctx_ckv.astype(BF16), ckv.reshape(nseq, L, kv_rank).astype(BF16)], axis=1)
        ck = ctx_kpe.astype(F32)
        ctx_kp = jnp.concatenate([ck, ck, jnp.zeros_like(ck), jnp.zeros_like(ck)], axis=-1)
        kpks_all = jnp.concatenate([ctx_kp, kpks.reshape(nseq, L, 2 * LANES)], axis=1)
        tk_ctx = jnp.concatenate([jnp.ones((past, LANES), F32), jnp.zeros((past, LANES), F32)], axis=-1)
        tk_all = jnp.concatenate([tk_ctx, jnp.concatenate([cos, cos, sin, sin], axis=-1)], axis=0)
        Lk = past + L
    else:
        ckv_all = ckv.reshape(nseq, L, kv_rank).astype(BF16)
        kpks_all = kpks.reshape(nseq, L, 2 * LANES)
        Lk = L
        tk_all = None
    tmk = _pick(Lk, (512, 256, 128))
    if tk_all is None:
        tk_all = jnp.concatenate([jnp.ones((tmk, LANES), F32), jnp.zeros((tmk, LANES), F32)], axis=-1)
    nk_t = tk_all.shape[0] // tmk
    Mk = nseq * Lk
    ckv_all = ckv_all.reshape(Mk, kv_rank)
    kcat = mm2d(ckv_all, wts['ukn_aug'], epilogue=_epi_kadd, tm=tmk,
                aux=[(kpks_all.reshape(Mk, 2 * LANES), (tmk, 2 * LANES), lambda g, i, j: (i, 0)),
                     (tk_all, (tmk, 2 * LANES), lambda g, i, j: (i % nk_t, 0))],
                out_dtype=BF16, name="mla_ukn")
    v = mm2d(ckv_all, wts['uv'], tm=tmk, out_dtype=BF16, name="mla_uv")
    att = attention(q.reshape(nseq, L, -1), kcat.reshape(nseq, Lk, -1), v.reshape(nseq, Lk, -1), heads)

    Dc = S5_CHUNK
    R = M // Dc
    n = L // Dc
    y_intra = mm(u2, w_t, name="s5_intra")
    g = mm(u2, w_b, name="s5_to_state")
    s_in, fin = s5_scan(g, ad, h0, n)
    tms = _pick(R, (512, 256, 128))
    tns = Dc * LANES
    gyb = mm(s_in, w_c, epilogue=_epi_add_gelu, tm=tms, tn=tns,
             aux=[(y_intra, (1, tms, tns), lambda g_, i, j: (g_, i, j))], out_mode="unfold_rows", fold=Dc,
             name="s5_from_state")
    s5w = nb * LANES
    tmg = _pick(M, (1024, 512, 256, 128))
    tng = _pick(s5w, (1024, 512, 256, 128))
    s5_out = mm(gyb, glu_w.astype(BF16)[None], epilogue=_epi_glu, a_blocked=True, tm=tmg, tn=tng,
                aux=[(gyb, (tng // LANES, tmg, LANES), lambda g_, i, j: (j, i, 0)),
                     (glu_b[None], (1, tng), lambda g_, i, j: (0, j))], out_dtype=BF16, name="s5_glu")[0]

    cat = jnp.concatenate([s5_out, att.reshape(M, -1)], axis=-1)
    y = mm2d(cat, wts['w_out'], out_dtype=BF16, name="ab_out")
    return y.reshape(nseq, L, D), ckv, kpks[:, :64], fin


def _pack_s5_state(re, im, nb):
    nseq = re.shape[0]
    def blk(x):
        return x.reshape(nseq, nb, -1).transpose(1, 0, 2)
    parts = [blk(re[:, 0]), blk(im[:, 0]), blk(re[:, 1]), blk(im[:, 1])]
    return jnp.concatenate(parts, axis=-1)[:, :, None, :].astype(F32)


def _unpack_s5_state(fin, groups, states):
    nb, nseq = fin.shape[0], fin.shape[1]
    half = fin.shape[-1] // 4
    def blk(x):
        return x.transpose(1, 0, 2).reshape(nseq, groups, states)
    f = fin[:, :, 0]
    re = jnp.stack([blk(f[..., 0:half]), blk(f[..., 2 * half:3 * half])], axis=1)
    im = jnp.stack([blk(f[..., half:2 * half]), blk(f[..., 3 * half:])], axis=1)
    return re, im


def _hg_mixer(h, w_in, w_out, j, lb, out_norm, s0, heads):
    nseq, L, D = h.shape
    M = nseq * L
    z = mm2d(h.reshape(M, D), w_in, wg=j, name="hg_in").reshape(nseq, L, -1)
    o_f, o_b, st = hgrn2(z, lb, s0, heads)
    o = hg_post(o_f, o_b, z, out_norm, heads)
    y = mm2d(o.reshape(M, -1), w_out, wg=j, out_dtype=BF16, name="hg_out")
    return y.reshape(nseq, L, D), st


def _mlp(h, w1, w2, layer):
    nseq, L, D = h.shape
    M = nseq * L
    z = mm2d(h.reshape(M, D), w1, wg=layer, epilogue=_epi_relu2, out_dtype=BF16, name="mlp_up")
    return mm2d(z, w2, wg=layer, out_dtype=BF16, name="mlp_down").reshape(nseq, L, D)


def kernel(x_prompt, x_sample, cache_ckv, cache_kpe, state_s5_re, state_s5_im, state_hgrn, c, c_ctx,
           mod_w, mod_b, norm_g, mlp_w1, mlp_w2, ab_w_in, ab_w_out, mla_q_norm, mla_kv_norm, mla_w_uq,
           mla_w_ukv, s5_log_dt, s5_lam_re, s5_lam_im, s5_b_re, s5_b_im, s5_c_re, s5_c_im, s5_d, s5_glu_w,
           s5_glu_b, hg_w_in, hg_w_out, hg_lower_bounds, hg_out_norm):
    depth = mod_w.shape[0]
    D = x_prompt.shape[-1]
    nsmp = x_sample.shape[0]
    s5w = s5_glu_w.shape[-1]
    q_rank = mla_q_norm.shape[-1]
    kv_rank = mla_kv_norm.shape[-1]
    mla_heads = (D - s5w) // 128
    hg_heads = hg_out_norm.shape[-1] // 128
    groups, states = s5_lam_re.shape[2], s5_lam_re.shape[3]

    lbs = jax.nn.softmax(hg_lower_bounds.astype(F32), axis=1)
    lbs = jnp.cumsum(lbs, axis=1) - lbs[:, :1]

    n_cond = nsmp + 1
    pad = (-n_cond) % 8
    cond = jnp.concatenate([c, c_ctx[None], jnp.zeros((pad, D), F32)], axis=0)

    mods = [modulation(cond, mod_w, mod_b, layer) for layer in range(depth)]
    w1_all, w2_all = mlp_w1.astype(BF16), mlp_w2.astype(BF16)
    hg_in_all, hg_out_all = hg_w_in.astype(BF16), hg_w_out.astype(BF16)

    xp, xs = x_prompt, x_sample
    hp = hs = None
    l_ckv, l_kpe, l_s5r, l_s5i, l_hg = [], [], [], [], []
    for layer in range(depth):
        j = layer // 2
        m = mods[layer]
        ms = [m[:nsmp, i * D:(i + 1) * D][:, None, :] for i in range(6)]
        mp = [m[nsmp:nsmp + 1, i * D:(i + 1) * D][:, None, :] for i in range(6)]
        if layer == 0:
            hp = norm_mod(xp, norm_g[layer, 0], mp[1], mp[0])
            hs = norm_mod(xs, norm_g[layer, 0], ms[1], ms[0])
        if layer % 2 == 0:
            wts = _ab_weights(ab_w_in[j], ab_w_out[j], mla_q_norm[j], mla_kv_norm[j], mla_w_uq[j], mla_w_ukv[j],
                              s5w, q_rank, kv_rank, mla_heads)
            s5m = _s5_weights(s5_log_dt[j], s5_lam_re[j], s5_lam_im[j], s5_b_re[j], s5_b_im[j],
                              s5_c_re[j], s5_c_im[j], s5_d[j])
            nb = s5m[0].shape[0]
            zero_h0 = jnp.zeros((nb, xp.shape[0], 1, s5m[3].shape[-1]), F32)
            yp, ckv, kpe, fin = _ab_mixer(hp, wts, s5m, s5_glu_w[j], s5_glu_b[j], heads=mla_heads,
                                          kv_rank=kv_rank, ctx_ckv=None, ctx_kpe=None, h0=zero_h0, rope=False)
            h0s = _pack_s5_state(state_s5_re[:, j], state_s5_im[:, j], nb)
            ys, _, _, _ = _ab_mixer(hs, wts, s5m, s5_glu_w[j], s5_glu_b[j], heads=mla_heads, kv_rank=kv_rank,
                                    ctx_ckv=cache_ckv[:, j], ctx_kpe=cache_kpe[:, j], h0=h0s, rope=True)
            hr, hi = _unpack_s5_state(fin, groups, states)
            l_ckv.append(ckv.reshape(xp.shape[0], xp.shape[1], kv_rank))
            l_kpe.append(kpe.reshape(xp.shape[0], xp.shape[1], 64))
            l_s5r.append(hr)
            l_s5i.append(hi)
        else:
            yp, st = _hg_mixer(hp, hg_in_all, hg_out_all, j, lbs[:, layer], hg_out_norm[j], None, hg_heads)
            ys, _ = _hg_mixer(hs, hg_in_all, hg_out_all, j, lbs[:, layer], hg_out_norm[j], state_hgrn[:, j],
                              hg_heads)
            l_hg.append(st)
        xp, hp = resid_norm(xp, yp, norm_g[layer, 1], mp[2], (norm_g[layer, 2], mp[4], mp[3]))
        xs, hs = resid_norm(xs, ys, norm_g[layer, 1], ms[2], (norm_g[layer, 2], ms[4], ms[3]))
        yp = _mlp(hp, w1_all, w2_all, layer)
        ys = _mlp(hs, w1_all, w2_all, layer)
        if layer + 1 < depth:
            m_n = mods[layer + 1]
            nxt_s = (norm_g[layer + 1, 0], m_n[:nsmp, D:2 * D][:, None, :], m_n[:nsmp, 0:D][:, None, :])
            nxt_p = (norm_g[layer + 1, 0], m_n[nsmp:nsmp + 1, D:2 * D][:, None, :],
                     m_n[nsmp:nsmp + 1, 0:D][:, None, :])
        else:
            nxt_s = nxt_p = None
        xp, hp = resid_norm(xp, yp, norm_g[layer, 3], mp[5], nxt_p)
        xs, hs = resid_norm(xs, ys, norm_g[layer, 3], ms[5], nxt_s)
    new_ckv = jnp.stack(l_ckv, axis=1)
    new_kpe = jnp.stack(l_kpe, axis=1)
    new_s5_re = jnp.stack(l_s5r, axis=1)
    new_s5_im = jnp.stack(l_s5i, axis=1)
    new_hgrn = jnp.stack(l_hg, axis=1)
    return (xp, xs, new_ckv, new_kpe, new_s5_re, new_s5_im, new_hgrn)
```

```python
import functools
import math

import jax
import jax.numpy as jnp
from jax import lax
from jax.experimental import pallas as pl
from jax.experimental.pallas import tpu as pltpu

F32 = jnp.float32
BF16 = jnp.bfloat16

EPS = 1e-6
GRID_W = 64
ROPE_BASE = 10000.0
LANES = 128
S5_CHUNK = 16
HG_SUB = 32
HG_BLOCK = 2 * HG_SUB
VMEM_LIMIT = 56 * 1024 * 1024
MM_VMEM_BUDGET = 42 * 1024 * 1024
ATT_HEADS = 2
HG_POST_HEADS = 8
HG_HEADS = 4
HG_BLOCKS = 16
HG_FOLD_MIN_BLOCKS = 8
S5_SEQS = 4


def _pick(n, prefs):
    for p in prefs:
        if n % p == 0:
            return p
    return n


def _cparams(sem):
    return pltpu.CompilerParams(dimension_semantics=sem, vmem_limit_bytes=VMEM_LIMIT)


def _sigmoid(x):
    return 0.5 + 0.5 * jnp.tanh(0.5 * x)


def _silu(x):
    return x * _sigmoid(x)


def _gelu_tanh(x):
    c = math.sqrt(2.0 / math.pi)
    return 0.5 * x * (1.0 + jnp.tanh(c * (x + 0.044715 * (x * x * x))))


def _rms(x):
    return x * lax.rsqrt(jnp.mean(x * x, axis=-1, keepdims=True) + EPS)


def _mm_kernel(*refs, nk, n_aux, epilogue, a_blocked, out_mode, fold):
    a_ref, w_ref = refs[0], refs[1]
    aux_refs = refs[2:2 + n_aux]
    o_ref = refs[2 + n_aux]
    scratch = list(refs[3 + n_aux:])

    def load_a():
        if a_blocked:
            return jnp.concatenate([a_ref[c] for c in range(a_ref.shape[0])], axis=-1).astype(BF16)
        return a_ref[0].astype(BF16)

    def finish(acc):
        aux = []
        for r in aux_refs:
            v = r[...]
            aux.append(v)
        out = epilogue(acc, *aux) if epilogue is not None else acc
        if out_mode == "fold_rows":
            slab_ref = scratch.pop()
            rows = slab_ref.shape[1] // fold
            for c in range(o_ref.shape[0]):
                slab_ref[c] = out[:, c * LANES:(c + 1) * LANES]
            for c in range(o_ref.shape[0]):
                for t in range(fold):
                    o_ref[c, :, t * LANES:(t + 1) * LANES] = (
                        slab_ref[c, pl.ds(t, rows, stride=fold), :].astype(o_ref.dtype))
        elif out_mode == "unfold_rows":
            rows = out.shape[0]
            for t in range(fold):
                o_ref[0, pl.ds(t, rows, stride=fold), :] = out[:, t * LANES:(t + 1) * LANES].astype(o_ref.dtype)
        else:
            o_ref[0] = out.astype(o_ref.dtype)

    if nk == 1:
        finish(jnp.dot(load_a(), w_ref[0].astype(BF16), preferred_element_type=F32))
        return

    acc_ref = scratch.pop(0)
    k = pl.program_id(3)

    @pl.when(k == 0)
    def _():
        acc_ref[...] = jnp.zeros_like(acc_ref)

    acc_ref[...] += jnp.dot(load_a(), w_ref[0].astype(BF16), preferred_element_type=F32)

    @pl.when(k == nk - 1)
    def _():
        finish(acc_ref[...])


def _pick_tk(K, tm, tn, a_bytes, w_bytes, out_bytes, aux_bytes):
    for tk in (K, 4096, 2048, 1024, 512, 256, 128):
        if tk > K or K % tk:
            continue
        acc = 0 if tk == K else 4 * tm * tn
        est = 2 * (tm * tk * a_bytes + tk * tn * w_bytes) + 2 * tm * tn * out_bytes + acc + 2 * aux_bytes
        if est <= MM_VMEM_BUDGET:
            return tk
    return LANES


def mm(a, w, *, wg=0, epilogue=None, aux=(), out_dtype=F32, tm=None, tn=None, tk=None,
       a_blocked=False, out_mode="plain", fold=1, name="mm"):
    _, K, N = w.shape
    G = 1 if a_blocked else a.shape[0]
    M = a.shape[1]
    tm = tm or _pick(M, (1024, 512, 256, 128))
    tn = tn or _pick(N, (1024, 512, 256, 128))
    if tk is None:
        aux_bytes = sum(math.prod(bshape) * arr.dtype.itemsize for arr, bshape, _ in aux)
        tk = _pick_tk(K, tm, tn, a.dtype.itemsize, w.dtype.itemsize, jnp.dtype(out_dtype).itemsize, aux_bytes)
    assert M % tm == 0 and N % tn == 0 and K % tk == 0, (M, N, K, tm, tn, tk)
    nk = K // tk
    grid = (G, M // tm, N // tn, nk)
    if a_blocked:
        a_spec = pl.BlockSpec((tk // LANES, tm, LANES), lambda g, i, j, k: (k, i, 0))
    else:
        a_spec = pl.BlockSpec((1, tm, tk), lambda g, i, j, k: (g, i, k))
    w_spec = pl.BlockSpec((1, tk, tn), lambda g, i, j, k: (wg + g, k, j))
    aux_arrays, aux_specs = [], []
    for arr, bshape, imap in aux:
        aux_arrays.append(arr)
        aux_specs.append(pl.BlockSpec(bshape, functools.partial(lambda g, i, j, k, f: f(g, i, j), f=imap)))
    scratch = [] if nk == 1 else [pltpu.VMEM((tm, tn), F32)]
    if out_mode == "fold_rows":
        assert G == 1 and tm % fold == 0
        out_shape = jax.ShapeDtypeStruct((N // LANES, M // fold, fold * LANES), out_dtype)
        out_spec = pl.BlockSpec((tn // LANES, tm // fold, fold * LANES), lambda g, i, j, k: (j, i, 0))
        scratch.append(pltpu.VMEM((tn // LANES, tm, LANES), F32))
    elif out_mode == "unfold_rows":
        assert tn == N == fold * LANES
        out_shape = jax.ShapeDtypeStruct((G, M * fold, LANES), out_dtype)
        out_spec = pl.BlockSpec((1, tm * fold, LANES), lambda g, i, j, k: (g, i, 0))
    else:
        out_shape = jax.ShapeDtypeStruct((G, M, N), out_dtype)
        out_spec = pl.BlockSpec((1, tm, tn), lambda g, i, j, k: (g, i, j))
    kern = functools.partial(_mm_kernel, nk=nk, n_aux=len(aux_arrays), epilogue=epilogue,
                             a_blocked=a_blocked, out_mode=out_mode, fold=fold)
    return pl.pallas_call(
        kern, grid=grid, in_specs=[a_spec, w_spec] + aux_specs, out_specs=out_spec,
        out_shape=out_shape, scratch_shapes=scratch, name=name,
        compiler_params=_cparams(("parallel", "parallel", "parallel", "arbitrary")),
    )(a, w, *aux_arrays)


def mm2d(a, w, **kw):
    out = mm(a[None], w if w.ndim == 3 else w[None], **kw)
    return out if kw.get("out_mode", "plain") != "plain" else out[0]


def _epi_relu2(acc):
    r = jnp.maximum(acc, 0.0)
    return r * r


def _epi_rms(acc, g):
    return _rms(acc) * g


def _epi_ckv(acc, g, *, rank):
    return jnp.concatenate([_rms(acc[:, :rank]) * g, acc[:, rank:]], axis=-1)


def _epi_table(acc, t):
    reps = acc.shape[1] // t.shape[1]
    return acc * jnp.concatenate([t] * reps, axis=-1)


def _epi_kadd(acc, kpks, t):
    kr2 = kpks[:, :LANES] * t[:, :LANES] + kpks[:, LANES:] * t[:, LANES:]
    blk = jnp.concatenate([jnp.zeros_like(kr2), kr2], axis=-1)
    reps = acc.shape[1] // blk.shape[1]
    return acc + jnp.concatenate([blk] * reps, axis=-1)


def _epi_add_gelu(acc, y0):
    return _gelu_tanh(acc + y0[0])


def _epi_glu(acc, gy, b):
    y = jnp.concatenate([gy[c] for c in range(gy.shape[0])], axis=-1).astype(F32)
    return y * _sigmoid(acc + b)


def _mod_kernel(c_ref, w_ref, b_ref, o_ref, *, nk):
    k = pl.program_id(1)

    @pl.when(k == 0)
    def _():
        o_ref[...] = jnp.zeros_like(o_ref)

    a = _silu(c_ref[...]).astype(BF16)
    o_ref[...] += jnp.dot(a, w_ref[...].astype(BF16), preferred_element_type=F32)

    @pl.when(k == nk - 1)
    def _():
        o_ref[...] += b_ref[...]


def modulation(cond, w, b, layer):
    rows, d = cond.shape
    n = w.shape[2]
    tn = _pick(n, (2048, 1024, 512, 256, 128))
    tk = _pick(d, (1024, 512, 256, 128))
    nk = d // tk
    return pl.pallas_call(
        functools.partial(_mod_kernel, nk=nk), grid=(n // tn, nk),
        in_specs=[pl.BlockSpec((rows, tk), lambda j, k: (0, k)),
                  pl.BlockSpec((None, tk, tn), lambda j, k: (layer, k, j)),
                  pl.BlockSpec((None, 1, tn), lambda j, k: (layer, 0, j))],
        out_specs=pl.BlockSpec((rows, tn), lambda j, k: (0, j)),
        out_shape=jax.ShapeDtypeStruct((rows, n), F32), name="modulation",
        compiler_params=_cparams(("parallel", "arbitrary")),
    )(cond, w, b[:, None, :])


def _norm_mod_kernel(x_ref, g_ref, sc_ref, sh_ref, o_ref):
    y = _rms(x_ref[0]) * g_ref[...]
    o_ref[0] = (y * (1.0 + sc_ref[0]) + sh_ref[0]).astype(o_ref.dtype)


def norm_mod(x, g, sc, sh):
    nseq, L, D = x.shape
    tr = _pick(L, (256, 128))
    per_seq = sc.shape[0] == nseq and nseq > 1
    smap = (lambda b, i: (b, 0, 0)) if per_seq else (lambda b, i: (0, 0, 0))
    return pl.pallas_call(
        _norm_mod_kernel, grid=(nseq, L // tr),
        in_specs=[pl.BlockSpec((1, tr, D), lambda b, i: (b, i, 0)),
                  pl.BlockSpec((1, D), lambda b, i: (0, 0)),
                  pl.BlockSpec((1, 1, D), smap), pl.BlockSpec((1, 1, D), smap)],
        out_specs=pl.BlockSpec((1, tr, D), lambda b, i: (b, i, 0)),
        out_shape=jax.ShapeDtypeStruct((nseq, L, D), BF16), name="norm_mod",
        compiler_params=_cparams(("parallel", "parallel")),
    )(x, g[None], sc, sh)


def _resid_kernel(x_ref, y_ref, g1_ref, gt_ref, *rest, with_h):
    xn = x_ref[0] + gt_ref[0] * (_rms(y_ref[0].astype(F32)) * g1_ref[...])
    if with_h:
        g2_ref, sc_ref, sh_ref, xo_ref, h_ref = rest
        xo_ref[0] = xn
        h_ref[0] = ((_rms(xn) * g2_ref[...]) * (1.0 + sc_ref[0]) + sh_ref[0]).astype(h_ref.dtype)
    else:
        (xo_ref,) = rest
        xo_ref[0] = xn


def resid_norm(x, y, g1, gate, nxt=None):
    nseq, L, D = x.shape
    tr = _pick(L, (256, 128))
    per_seq = gate.shape[0] == nseq and nseq > 1
    smap = (lambda b, i: (b, 0, 0)) if per_seq else (lambda b, i: (0, 0, 0))
    row = pl.BlockSpec((1, tr, D), lambda b, i: (b, i, 0))
    vec = pl.BlockSpec((1, D), lambda b, i: (0, 0))
    mod = pl.BlockSpec((1, 1, D), smap)
    args = [x, y, g1[None], gate]
    specs = [row, row, vec, mod]
    out_shape = [jax.ShapeDtypeStruct((nseq, L, D), F32)]
    out_specs = [row]
    if nxt is not None:
        g2, sc, sh = nxt
        args += [g2[None], sc, sh]
        specs += [vec, mod, mod]
        out_shape.append(jax.ShapeDtypeStruct((nseq, L, D), BF16))
        out_specs.append(row)
    res = pl.pallas_call(
        functools.partial(_resid_kernel, with_h=nxt is not None), grid=(nseq, L // tr),
        in_specs=specs, out_specs=out_specs, out_shape=out_shape, name="resid_norm",
        compiler_params=_cparams(("parallel", "parallel")),
    )(*args)
    return (res[0], res[1]) if nxt is not None else (res[0], None)


def _attn_kernel(q_ref, k_ref, v_ref, o_ref, *, hp):
    scores = []
    for h in range(hp):
        qk = slice(h * 2 * LANES, (h + 1) * 2 * LANES)
        scores.append(lax.dot_general(q_ref[0, :, qk], k_ref[0, :, qk], (((1,), (1,)), ((), ())),
                                      preferred_element_type=F32))
    probs = []
    for s in scores:
        m = jnp.max(s, axis=-1, keepdims=True)
        p = jnp.exp2(s - m)
        probs.append((p.astype(BF16), jnp.sum(p, axis=-1, keepdims=True)))
    for h, (p, l) in enumerate(probs):
        vo = slice(h * LANES, (h + 1) * LANES)
        o = jnp.dot(p, v_ref[0, :, vo], preferred_element_type=F32)
        o_ref[0, :, vo] = (o / l).astype(o_ref.dtype)


def attention(q, kcat, v, heads):
    nseq, L, _ = q.shape
    Lk = kcat.shape[1]
    tq = _pick(L, (256, 128))
    hp = ATT_HEADS if heads % ATT_HEADS == 0 else 1
    return pl.pallas_call(
        functools.partial(_attn_kernel, hp=hp), grid=(nseq, heads // hp, L // tq),
        in_specs=[pl.BlockSpec((1, tq, hp * 2 * LANES), lambda b, h, i: (b, i, h)),
                  pl.BlockSpec((1, Lk, hp * 2 * LANES), lambda b, h, i: (b, 0, h)),
                  pl.BlockSpec((1, Lk, hp * LANES), lambda b, h, i: (b, 0, h))],
        out_specs=pl.BlockSpec((1, tq, hp * LANES), lambda b, h, i: (b, i, h)),
        out_shape=jax.ShapeDtypeStruct((nseq, L, heads * LANES), BF16), name="mla_attention",
        compiler_params=_cparams(("parallel", "parallel", "arbitrary")),
    )(q, kcat, v)


def _s5_scan_kernel(g_ref, ad_ref, h0_ref, sin_ref, fin_ref, *, n, sb, half):
    ad = ad_ref[0]
    afr, afi = ad[:, 0:half], ad[:, half:2 * half]
    abr, abi = ad[:, 2 * half:3 * half], ad[:, 3 * half:4 * half]

    def body(k, carry):
        out = []
        for s in range(sb):
            fr, fi, br, bi = carry[4 * s:4 * s + 4]
            row_f = pl.ds(s * n + k, 1)
            row_b = pl.ds(s * n + n - 1 - k, 1)
            sin_ref[0, row_f, 0:2 * half] = jnp.concatenate([fr, fi], axis=-1)
            sin_ref[0, row_b, 2 * half:4 * half] = jnp.concatenate([br, bi], axis=-1)
            gf = g_ref[0, row_f, 0:2 * half]
            gb = g_ref[0, row_b, 2 * half:4 * half]
            out += [afr * fr - afi * fi + gf[:, :half], afr * fi + afi * fr + gf[:, half:],
                    abr * br - abi * bi + gb[:, :half], abr * bi + abi * br + gb[:, half:]]
        return tuple(out)

    init = []
    for s in range(sb):
        h0 = h0_ref[0, s]
        init += [h0[:, 0:half], h0[:, half:2 * half], h0[:, 2 * half:3 * half], h0[:, 3 * half:4 * half]]
    fin = lax.fori_loop(0, n, body, tuple(init))
    for s in range(sb):
        fin_ref[0, s] = jnp.concatenate(fin[4 * s:4 * s + 4], axis=-1)


def s5_scan(g, ad, h0, n):
    nb, rows, W = g.shape
    nseq = rows // n
    sb = S5_SEQS if nseq % S5_SEQS == 0 else 1
    return pl.pallas_call(
        functools.partial(_s5_scan_kernel, n=n, sb=sb, half=W // 4), grid=(nb, nseq // sb),
        in_specs=[pl.BlockSpec((1, sb * n, W), lambda j, b: (j, b, 0)),
                  pl.BlockSpec((1, 1, W), lambda j, b: (j, 0, 0)),
                  pl.BlockSpec((1, sb, 1, W), lambda j, b: (j, b, 0, 0))],
        out_specs=[pl.BlockSpec((1, sb * n, W), lambda j, b: (j, b, 0)),
                   pl.BlockSpec((1, sb, 1, W), lambda j, b: (j, b, 0, 0))],
        out_shape=[jax.ShapeDtypeStruct((nb, rows, W), F32),
                   jax.ShapeDtypeStruct((nb, nseq, 1, W), F32)], name="s5_scan",
        compiler_params=_cparams(("parallel", "parallel")),
    )(g, ad, h0)


def _hg_masks(rev):
    T = HG_BLOCK
    row = lax.broadcasted_iota(jnp.int32, (T, LANES), 0)
    first = row < HG_SUB
    r2 = lax.broadcasted_iota(jnp.int32, (T, 2 * T), 0)
    c2 = lax.broadcasted_iota(jnp.int32, (T, 2 * T), 1)
    s2 = c2 & (T - 1)
    sub_bits = HG_SUB.bit_length() - 1
    other_sub = (r2 ^ s2) >> sub_bits
    causal = (s2 >= r2) if rev else (s2 <= r2)
    keep = jnp.where(other_sub == (c2 >> (sub_bits + 1)), jnp.where(causal, 1, 0), 0) > 0
    rin = row & (HG_SUB - 1)
    steps = (1, 2, 4, 8, 16)
    scan = [(rin < HG_SUB - s) if rev else (rin >= s) for s in steps]
    return steps, scan, first, keep


def _hg_direction(q, z, v, lb, st, rev, masks, fold):
    T = HG_BLOCK
    steps, scan, first, keep = masks
    qa = _silu(q) * (LANES ** -0.5)
    th = 0.5 * jnp.tanh(0.5 * z)
    sig = 0.5 + th
    nsig = 0.5 - th
    kk = (1.0 - lb) * nsig
    g = jnp.log(lb + (1.0 - lb) * sig)

    b = g
    for s, ok in zip(steps, scan):
        b = b + jnp.where(ok, pltpu.roll(b, (T - s) if rev else s, axis=0), 0.0)
    if not rev:
        b0, b1 = b[HG_SUB - 1:HG_SUB], b[T - 1:T]
    else:
        b0, b1 = b[0:1], b[HG_SUB:HG_SUB + 1]
    bsub = jnp.where(first, b0, b1)
    qh = qa * jnp.exp(b)
    kd = kk * jnp.exp(-b)
    ke = kk * jnp.exp(bsub - b)

    kcat = jnp.concatenate([kd, ke], axis=0).astype(BF16)
    att = lax.dot_general(qh.astype(BF16), kcat, (((1,), (1,)), ((), ())), preferred_element_type=F32)
    att = jnp.where(keep, att, 0.0)
    vb = v.astype(BF16)
    if fold:
        att = att + pltpu.roll(att, T, axis=1)
        o = jnp.dot(att[:, :T].astype(BF16), vb, preferred_element_type=F32)
    else:
        o = jnp.dot(att.astype(BF16), jnp.concatenate([vb, vb], axis=0), preferred_element_type=F32)

    if not rev:
        dq = jnp.where(first, 1.0, jnp.exp(b0))
        ek = jnp.where(first, jnp.exp(b1), 1.0)
    else:
        dq = jnp.where(first, jnp.exp(b1), 1.0)
        ek = jnp.where(first, 1.0, jnp.exp(b0))
    o = o + lax.dot_general((qh * dq).astype(BF16), st.astype(BF16), (((1,), (1,)), ((), ())),
                            preferred_element_type=F32)
    upd = lax.dot_general(vb, (ke * ek).astype(BF16), (((0,), (0,)), ((), ())), preferred_element_type=F32)
    st_new = st * jnp.exp(b0 + b1) + upd
    return o, st_new


def _hg_kernel(*refs, nsteps, nb, hp, has_init):
    if has_init:
        (qf_ref, zf_ref, vf_ref, qb_ref, zb_ref, vb_ref, lb_ref, s0_ref,
         of_ref, ob_ref, so_ref, stf_ref, stb_ref) = refs
    else:
        (qf_ref, zf_ref, vf_ref, qb_ref, zb_ref, vb_ref, lb_ref,
         of_ref, ob_ref, so_ref, stf_ref, stb_ref) = refs
    i = pl.program_id(2)
    T = HG_BLOCK

    @pl.when(i == 0)
    def _():
        for h in range(hp):
            if has_init:
                stf_ref[h] = s0_ref[0, 0, h].T
                stb_ref[h] = s0_ref[0, 1, h].T
            else:
                stf_ref[h] = jnp.zeros((LANES, LANES), F32)
                stb_ref[h] = jnp.zeros((LANES, LANES), F32)

    lb = lb_ref[...]
    masks_f = _hg_masks(False)
    masks_b = _hg_masks(True)
    for h in range(hp):
        lanes = slice(h * LANES, (h + 1) * LANES)
        st_f = stf_ref[h]
        st_b = stb_ref[h]
        for blk in range(nb):
            rows = slice(blk * T, (blk + 1) * T)
            o_f, st_f = _hg_direction(qf_ref[0, rows, lanes], zf_ref[0, rows, lanes], vf_ref[0, rows, lanes],
                                      lb[0:1, lanes], st_f, False, masks_f, nb >= HG_FOLD_MIN_BLOCKS)
            of_ref[0, rows, lanes] = o_f.astype(of_ref.dtype)
        for blk in reversed(range(nb)):
            rows = slice(blk * T, (blk + 1) * T)
            o_b, st_b = _hg_direction(qb_ref[0, rows, lanes], zb_ref[0, rows, lanes], vb_ref[0, rows, lanes],
                                      lb[1:2, lanes], st_b, True, masks_b, nb >= HG_FOLD_MIN_BLOCKS)
            ob_ref[0, rows, lanes] = o_b.astype(ob_ref.dtype)
        stf_ref[h] = st_f
        stb_ref[h] = st_b

        @pl.when(i == nsteps - 1)
        def _(h=h, st_f=st_f, st_b=st_b):
            so_ref[0, 0, h] = st_f.T
            so_ref[0, 1, h] = st_b.T


def hgrn2(z, lb, s0, heads):
    nseq, L, _ = z.shape
    nb = max(n for n in range(1, HG_BLOCKS + 1) if L % (n * HG_BLOCK) == 0)
    hp = HG_HEADS if heads % HG_HEADS == 0 else 1
    T = nb * HG_BLOCK
    nsteps = L // T
    HB = heads // hp

    def col(off, rev):
        if rev:
            return lambda b, h, i: (b, nsteps - 1 - i, off + h)
        return lambda b, h, i: (b, i, off + h)

    tile = lambda off, rev: pl.BlockSpec((1, T, hp * LANES), col(off, rev))
    in_specs = [tile(0, False), tile(HB, False), tile(3 * HB, False),
                tile(0, True), tile(2 * HB, True), tile(3 * HB, True),
                pl.BlockSpec((2, hp * LANES), lambda b, h, i: (0, h))]
    args = [z, z, z, z, z, z, lb]
    st_spec = pl.BlockSpec((1, 2, hp, LANES, LANES), lambda b, h, i: (b, 0, h, 0, 0))
    if s0 is not None:
        in_specs.append(st_spec)
        args.append(s0)
    o_shape = jax.ShapeDtypeStruct((nseq, L, heads * LANES), BF16)
    st_scratch = pltpu.VMEM((hp, LANES, LANES), F32)
    return pl.pallas_call(
        functools.partial(_hg_kernel, nsteps=nsteps, nb=nb, hp=hp, has_init=s0 is not None),
        grid=(nseq, HB, nsteps), in_specs=in_specs,
        out_specs=[tile(0, False), tile(0, True), st_spec],
        out_shape=[o_shape, o_shape, jax.ShapeDtypeStruct((nseq, 2, heads, LANES, LANES), F32)],
        scratch_shapes=[st_scratch, st_scratch], name="hgrn2",
        compiler_params=_cparams(("parallel", "parallel", "arbitrary")),
    )(*args)


def _hg_post_kernel(of_ref, ob_ref, g_ref, gn_ref, o_ref, *, hp):
    for h in range(hp):
        lanes = slice(h * LANES, (h + 1) * LANES)
        o = of_ref[0, :, lanes].astype(F32) + ob_ref[0, :, lanes].astype(F32)
        o_ref[0, :, lanes] = ((_rms(o) * gn_ref[:, lanes]) * _silu(g_ref[0, :, lanes])).astype(o_ref.dtype)


def hg_post(o_f, o_b, z, out_norm, heads):
    nseq, L, _ = o_f.shape
    tr = _pick(L, (256, 128))
    hp = HG_POST_HEADS if heads % HG_POST_HEADS == 0 else 1
    tile = pl.BlockSpec((1, tr, hp * LANES), lambda b, i, h: (b, i, h))
    return pl.pallas_call(
        functools.partial(_hg_post_kernel, hp=hp), grid=(nseq, L // tr, heads // hp),
        in_specs=[tile, tile, pl.BlockSpec((1, tr, hp * LANES), lambda b, i, h: (b, i, 4 * (heads // hp) + h)),
                  pl.BlockSpec((1, hp * LANES), lambda b, i, h: (0, h))],
        out_specs=tile, out_shape=jax.ShapeDtypeStruct(o_f.shape, BF16), name="hg_post",
        compiler_params=_cparams(("parallel", "parallel", "parallel")),
    )(o_f, o_b, z, out_norm[None])


def _rope_tables(n_l):
    rope = 64
    half = rope // 2
    rows = n_l // GRID_W
    row = jnp.repeat(jnp.arange(rows), GRID_W).astype(F32)
    col = jnp.tile(jnp.arange(GRID_W), rows).astype(F32)
    inv = ROPE_BASE ** (-jnp.arange(0, half, 2, dtype=F32) / half)
    ar = row[:, None] * inv[None]
    ac = col[:, None] * inv[None]
    cos = jnp.concatenate([jnp.cos(ar), jnp.cos(ar), jnp.cos(ac), jnp.cos(ac)], axis=-1)
    sin = jnp.concatenate([jnp.sin(ar), jnp.sin(ar), jnp.sin(ac), jnp.sin(ac)], axis=-1)
    return cos, sin


def _rot_cols(w):
    return jnp.concatenate([-w[..., 16:32], w[..., 0:16], -w[..., 48:64], w[..., 32:48]], axis=-1)


def _s5_weights(log_dt, lam_re, lam_im, b_re, b_im, c_re, c_im, d_skip):
    D = S5_CHUNK
    G, P = lam_re.shape[1], lam_re.shape[2]
    C = b_re.shape[-1]
    gl = LANES // C
    nb = G // gl
    dt = jnp.exp(log_dt.astype(F32))[..., None]
    lr, li = lam_re.astype(F32), lam_im.astype(F32)
    mag = jnp.exp(lr * dt)
    ar, ai = mag * jnp.cos(li * dt), mag * jnp.sin(li * dt)
    den = lr * lr + li * li
    cr_ = ((ar - 1.0) * lr + ai * li) / den
    ci_ = (ai * lr - (ar - 1.0) * li) / den
    br, bi = b_re.astype(F32), b_im.astype(F32)
    bbr = cr_[..., None] * br - ci_[..., None] * bi
    bbi = cr_[..., None] * bi + ci_[..., None] * br
    cr, ci = c_re.astype(F32), c_im.astype(F32)
    def powers(n):
        n = n.astype(F32)[:, None, None, None]
        pmag = jnp.exp(n * (lr * dt)[None])
        return pmag * jnp.cos(n * (li * dt)[None]), pmag * jnp.sin(n * (li * dt)[None])

    def times_b(p_r, p_i):
        return (p_r[..., None] * bbr[None] - p_i[..., None] * bbi[None],
                p_r[..., None] * bbi[None] + p_i[..., None] * bbr[None])

    steps = jnp.arange(D)
    pr, pi = powers(jnp.arange(D + 1))
    abr, abi = times_b(pr[:D], pi[:D])
    abr_dn, abi_dn = times_b(*powers(D - 1 - steps))
    pr_dn, pi_dn = powers(D - steps)
    half = gl * P
    lane_group = jnp.arange(LANES) // C
    tok_lane_group = jnp.tile(lane_group, D)
    state_group = jnp.arange(half) // P

    def response(x, a_r, a_i):
        return jnp.einsum('gcp,ngpk->ngkc', cr[x], a_r) - jnp.einsum('gcp,ngpk->ngkc', ci[x], a_i)

    kf = response(0, abr[:, 0], abi[:, 0])
    kb = response(1, abr_dn[:, 1], abi_dn[:, 1])
    skip = d_skip.astype(F32)[:, :, None] * jnp.eye(C, dtype=F32)[None]
    zpad = jnp.zeros((D - 1,) + kf.shape[1:], F32)
    lagk = (jnp.concatenate([zpad, kf], axis=0) + jnp.concatenate([kb, zpad], axis=0)
            + jnp.concatenate([zpad, skip[None], zpad], axis=0))
    bd = jnp.tile(lagk.reshape(2 * D - 1, nb, LANES, C), (1, 1, 1, gl))
    bd = jnp.where(lane_group[:, None] == lane_group[None, :], bd, 0.0).astype(BF16)
    lag_idx = jnp.arange(D)[None, :] - jnp.arange(D)[:, None] + (D - 1)
    w_t = bd[lag_idx].transpose(2, 0, 3, 1, 4).reshape(nb, D * LANES, D * LANES)

    def to_state(x):
        rows = x.reshape(D, nb, gl, P, C).transpose(1, 0, 2, 4, 3).reshape(nb, D * LANES, P)
        tiled = jnp.tile(rows, (1, 1, gl))
        return jnp.where(tok_lane_group[:, None] == state_group[None, :], tiled, 0.0).astype(BF16)

    w_b = jnp.concatenate([to_state(abr_dn[:, 0]), to_state(abi_dn[:, 0]),
                           to_state(abr[:, 1]), to_state(abi[:, 1])], axis=-1)

    def from_state(y):
        cols = y.reshape(D, nb, gl, C, P).transpose(1, 4, 0, 2, 3).reshape(nb, P, D * LANES)
        tiled = jnp.tile(cols, (1, gl, 1))
        return jnp.where(state_group[:, None] == tok_lane_group[None, :], tiled, 0.0).astype(BF16)

    def c_times(x, p_r, p_i):
        re = cr[x][None] * p_r[:, :, None, :] - ci[x][None] * p_i[:, :, None, :]
        im = cr[x][None] * p_i[:, :, None, :] + ci[x][None] * p_r[:, :, None, :]
        return re, -im

    cf = c_times(0, pr[1:D + 1, 0], pi[1:D + 1, 0])
    cb = c_times(1, pr_dn[:, 1], pi_dn[:, 1])
    w_c = jnp.concatenate([from_state(cf[0]), from_state(cf[1]), from_state(cb[0]), from_state(cb[1])], axis=1)
    adr = pr[D].reshape(2, nb, half)
    adi = pi[D].reshape(2, nb, half)
    ad = jnp.concatenate([adr[0], adi[0], adr[1], adi[1]], axis=-1)[:, None, :]
    return w_t, w_b, w_c, ad


def _ab_weights(w_in, w_out, q_norm, kv_norm, w_uq, w_ukv, s5w, q_rank, kv_rank, heads):
    rope, nope, vdim = 64, 128, 128
    o1, o2, o3 = s5w, s5w + q_rank, s5w + q_rank + kv_rank
    w_u = w_in[:, :o1].astype(BF16)
    w_q = w_in[:, o1:o2].astype(BF16)
    w_kp = w_in[:, o3:]
    w_ks = _rot_cols(w_kp)
    w_small = jnp.concatenate([w_in[:, o2:o3], w_kp, w_kp, w_ks, w_ks], axis=-1).astype(BF16)
    uq = w_uq.reshape(q_rank, heads, nope + rope)
    uq_aug = jnp.concatenate([uq[..., :nope], uq[..., nope:], _rot_cols(uq[..., nope:])], axis=-1)
    uq_aug = uq_aug.reshape(q_rank, heads * 2 * LANES).astype(BF16)
    ukv = w_ukv.reshape(kv_rank, heads, nope + vdim)
    ukn_aug = jnp.concatenate([ukv[..., :nope], jnp.zeros_like(ukv[..., :nope])], axis=-1)
    ukn_aug = ukn_aug.reshape(kv_rank, heads * 2 * LANES).astype(BF16)
    uv = ukv[..., nope:].reshape(kv_rank, heads * vdim).astype(BF16)
    return dict(w_u=w_u, w_q=w_q, w_small=w_small, uq_aug=uq_aug, ukn_aug=ukn_aug, uv=uv,
                w_out=w_out.astype(BF16), q_norm=q_norm[None], kv_norm=kv_norm[None])


def _ab_mixer(h, wts, s5m, glu_w, glu_b, *, heads, kv_rank, ctx_ckv, ctx_kpe, h0, rope):
    nseq, L, D = h.shape
    M = nseq * L
    h2 = h.reshape(M, D)
    w_t, w_b, w_c, ad = s5m
    nb = w_t.shape[0]
    scale = (128 + 64) ** -0.5 * math.log2(math.e)

    u2 = mm2d(h2, wts['w_u'], out_dtype=BF16, out_mode="fold_rows", fold=S5_CHUNK, name="ab_in_u")
    tmq = _pick(L, (1024, 512, 256, 128))
    qlat = mm2d(h2, wts['w_q'], epilogue=_epi_rms, tn=wts['w_q'].shape[1], tm=_pick(M, (512, 256, 128)),
                aux=[(wts['q_norm'], (1, wts['w_q'].shape[1]), lambda g, i, j: (0, 0))],
                out_dtype=BF16, name="ab_in_q")
    nsm = wts['w_small'].shape[1]
    small = mm2d(h2, wts['w_small'], epilogue=functools.partial(_epi_ckv, rank=kv_rank), tn=nsm,
                 tm=_pick(M, (512, 256, 128)),
                 aux=[(wts['kv_norm'], (1, kv_rank), lambda g, i, j: (0, 0))], name="ab_in_kv")
    ckv = small[:, :kv_rank]
    kpks = small[:, kv_rank:]

    ones = jnp.ones((tmq, 64), F32)
    zeros = jnp.zeros((tmq, 64), F32)
    if rope:
        cos, sin = _rope_tables(L)
    else:
        cos, sin = ones, zeros
    one128 = jnp.ones((cos.shape[0], LANES), F32)
    tq = jnp.concatenate([one128, cos, sin], axis=-1) * scale
    nq_t = tq.shape[0] // tmq
    q = mm2d(qlat, wts['uq_aug'], epilogue=_epi_table, tm=tmq,
             aux=[(tq, (tmq, 2 * LANES), lambda g, i, j: (i % nq_t, 0))], out_dtype=BF16, name="mla_uq")

    if ctx_ckv is not None:
        past = ctx_ckv.shape[1]
        ckv_all = jnp.concatenate([ctx_ckv.astype(BF16), ckv.reshape(nseq, L, kv_rank).astype(BF16)], axis=1)
        ck = ctx_kpe.astype(F32)
        ctx_kp = jnp.concatenate([ck, ck, jnp.zeros_like(ck), jnp.zeros_like(ck)], axis=-1)
        kpks_all = jnp.concatenate([ctx_kp, kpks.reshape(nseq, L, 2 * LANES)], axis=1)
        tk_ctx = jnp.concatenate([jnp.ones((past, LANES), F32), jnp.zeros((past, LANES), F32)], axis=-1)
        tk_all = jnp.concatenate([tk_ctx, jnp.concatenate([cos, cos, sin, sin], axis=-1)], axis=0)
        Lk = past + L
    else:
        ckv_all = ckv.reshape(nseq, L, kv_rank).astype(BF16)
        kpks_all = kpks.reshape(nseq, L, 2 * LANES)
        Lk = L
        tk_all = None
    tmk = _pick(Lk, (512, 256, 128))
    if tk_all is None:
        tk_all = jnp.concatenate([jnp.ones((tmk, LANES), F32), jnp.zeros((tmk, LANES), F32)], axis=-1)
    nk_t = tk_all.shape[0] // tmk
    Mk = nseq * Lk
    ckv_all = ckv_all.reshape(Mk, kv_rank)
    kcat = mm2d(ckv_all, wts['ukn_aug'], epilogue=_epi_kadd, tm=tmk,
                aux=[(kpks_all.reshape(Mk, 2 * LANES), (tmk, 2 * LANES), lambda g, i, j: (i, 0)),
                     (tk_all, (tmk, 2 * LANES), lambda g, i, j: (i % nk_t, 0))],
                out_dtype=BF16, name="mla_ukn")
    v = mm2d(ckv_all, wts['uv'], tm=tmk, out_dtype=BF16, name="mla_uv")
    att = attention(q.reshape(nseq, L, -1), kcat.reshape(nseq, Lk, -1), v.reshape(nseq, Lk, -1), heads)

    Dc = S5_CHUNK
    R = M // Dc
    n = L // Dc
    y_intra = mm(u2, w_t, name="s5_intra")
    g = mm(u2, w_b, name="s5_to_state")
    s_in, fin = s5_scan(g, ad, h0, n)
    tms = _pick(R, (512, 256, 128))
    tns = Dc * LANES
    gyb = mm(s_in, w_c, epilogue=_epi_add_gelu, tm=tms, tn=tns,
             aux=[(y_intra, (1, tms, tns), lambda g_, i, j: (g_, i, j))], out_mode="unfold_rows", fold=Dc,
             name="s5_from_state")
    s5w = nb * LANES
    tmg = _pick(M, (1024, 512, 256, 128))
    tng = _pick(s5w, (1024, 512, 256, 128))
    s5_out = mm(gyb, glu_w.astype(BF16)[None], epilogue=_epi_glu, a_blocked=True, tm=tmg, tn=tng,
                aux=[(gyb, (tng // LANES, tmg, LANES), lambda g_, i, j: (j, i, 0)),
                     (glu_b[None], (1, tng), lambda g_, i, j: (0, j))], out_dtype=BF16, name="s5_glu")[0]

    cat = jnp.concatenate([s5_out, att.reshape(M, -1)], axis=-1)
    y = mm2d(cat, wts['w_out'], out_dtype=BF16, name="ab_out")
    return y.reshape(nseq, L, D), ckv, kpks[:, :64], fin


def _pack_s5_state(re, im, nb):
    nseq = re.shape[0]
    def blk(x):
        return x.reshape(nseq, nb, -1).transpose(1, 0, 2)
    parts = [blk(re[:, 0]), blk(im[:, 0]), blk(re[:, 1]), blk(im[:, 1])]
    return jnp.concatenate(parts, axis=-1)[:, :, None, :].astype(F32)


def _unpack_s5_state(fin, groups, states):
    nb, nseq = fin.shape[0], fin.shape[1]
    half = fin.shape[-1] // 4
    def blk(x):
        return x.transpose(1, 0, 2).reshape(nseq, groups, states)
    f = fin[:, :, 0]
    re = jnp.stack([blk(f[..., 0:half]), blk(f[..., 2 * half:3 * half])], axis=1)
    im = jnp.stack([blk(f[..., half:2 * half]), blk(f[..., 3 * half:])], axis=1)
    return re, im


def _hg_mixer(h, w_in, w_out, j, lb, out_norm, s0, heads):
    nseq, L, D = h.shape
    M = nseq * L
    z = mm2d(h.reshape(M, D), w_in, wg=j, name="hg_in").reshape(nseq, L, -1)
    o_f, o_b, st = hgrn2(z, lb, s0, heads)
    o = hg_post(o_f, o_b, z, out_norm, heads)
    y = mm2d(o.reshape(M, -1), w_out, wg=j, out_dtype=BF16, name="hg_out")
    return y.reshape(nseq, L, D), st


def _mlp(h, w1, w2, layer):
    nseq, L, D = h.shape
    M = nseq * L
    z = mm2d(h.reshape(M, D), w1, wg=layer, epilogue=_epi_relu2, out_dtype=BF16, name="mlp_up")
    return mm2d(z, w2, wg=layer, out_dtype=BF16, name="mlp_down").reshape(nseq, L, D)


def kernel(x_prompt, x_sample, cache_ckv, cache_kpe, state_s5_re, state_s5_im, state_hgrn, c, c_ctx,
           mod_w, mod_b, norm_g, mlp_w1, mlp_w2, ab_w_in, ab_w_out, mla_q_norm, mla_kv_norm, mla_w_uq,
           mla_w_ukv, s5_log_dt, s5_lam_re, s5_lam_im, s5_b_re, s5_b_im, s5_c_re, s5_c_im, s5_d, s5_glu_w,
           s5_glu_b, hg_w_in, hg_w_out, hg_lower_bounds, hg_out_norm):
    depth = mod_w.shape[0]
    D = x_prompt.shape[-1]
    nsmp = x_sample.shape[0]
    s5w = s5_glu_w.shape[-1]
    q_rank = mla_q_norm.shape[-1]
    kv_rank = mla_kv_norm.shape[-1]
    mla_heads = (D - s5w) // 128
    hg_heads = hg_out_norm.shape[-1] // 128
    groups, states = s5_lam_re.shape[2], s5_lam_re.shape[3]

    lbs = jax.nn.softmax(hg_lower_bounds.astype(F32), axis=1)
    lbs = jnp.cumsum(lbs, axis=1) - lbs[:, :1]

    n_cond = nsmp + 1
    pad = (-n_cond) % 8
    cond = jnp.concatenate([c, c_ctx[None], jnp.zeros((pad, D), F32)], axis=0)

    mods = [modulation(cond, mod_w, mod_b, layer) for layer in range(depth)]
    w1_all, w2_all = mlp_w1.astype(BF16), mlp_w2.astype(BF16)
    hg_in_all, hg_out_all = hg_w_in.astype(BF16), hg_w_out.astype(BF16)

    xp, xs = x_prompt, x_sample
    hp = hs = None
    l_ckv, l_kpe, l_s5r, l_s5i, l_hg = [], [], [], [], []
    for layer in range(depth):
        j = layer // 2
        m = mods[layer]
        ms = [m[:nsmp, i * D:(i + 1) * D][:, None, :] for i in range(6)]
        mp = [m[nsmp:nsmp + 1, i * D:(i + 1) * D][:, None, :] for i in range(6)]
        if layer == 0:
            hp = norm_mod(xp, norm_g[layer, 0], mp[1], mp[0])
            hs = norm_mod(xs, norm_g[layer, 0], ms[1], ms[0])
        if layer % 2 == 0:
            wts = _ab_weights(ab_w_in[j], ab_w_out[j], mla_q_norm[j], mla_kv_norm[j], mla_w_uq[j], mla_w_ukv[j],
                              s5w, q_rank, kv_rank, mla_heads)
            s5m = _s5_weights(s5_log_dt[j], s5_lam_re[j], s5_lam_im[j], s5_b_re[j], s5_b_im[j],
                              s5_c_re[j], s5_c_im[j], s5_d[j])
            nb = s5m[0].shape[0]
            zero_h0 = jnp.zeros((nb, xp.shape[0], 1, s5m[3].shape[-1]), F32)
            yp, ckv, kpe, fin = _ab_mixer(hp, wts, s5m, s5_glu_w[j], s5_glu_b[j], heads=mla_heads,
                                          kv_rank=kv_rank, ctx_ckv=None, ctx_kpe=None, h0=zero_h0, rope=False)
            h0s = _pack_s5_state(state_s5_re[:, j], state_s5_im[:, j], nb)
            ys, _, _, _ = _ab_mixer(hs, wts, s5m, s5_glu_w[j], s5_glu_b[j], heads=mla_heads, kv_rank=kv_rank,
                                    ctx_ckv=cache_ckv[:, j], ctx_kpe=cache_kpe[:, j], h0=h0s, rope=True)
            hr, hi = _unpack_s5_state(fin, groups, states)
            l_ckv.append(ckv.reshape(xp.shape[0], xp.shape[1], kv_rank))
            l_kpe.append(kpe.reshape(xp.shape[0], xp.shape[1], 64))
            l_s5r.append(hr)
            l_s5i.append(hi)
        else:
            yp, st = _hg_mixer(hp, hg_in_all, hg_out_all, j, lbs[:, layer], hg_out_norm[j], None, hg_heads)
            ys, _ = _hg_mixer(hs, hg_in_all, hg_out_all, j, lbs[:, layer], hg_out_norm[j], state_hgrn[:, j],
                              hg_heads)
            l_hg.append(st)
        xp, hp = resid_norm(xp, yp, norm_g[layer, 1], mp[2], (norm_g[layer, 2], mp[4], mp[3]))
        xs, hs = resid_norm(xs, ys, norm_g[layer, 1], ms[2], (norm_g[layer, 2], ms[4], ms[3]))
        yp = _mlp(hp, w1_all, w2_all, layer)
        ys = _mlp(hs, w1_all, w2_all, layer)
        if layer + 1 < depth:
            m_n = mods[layer + 1]
            nxt_s = (norm_g[layer + 1, 0], m_n[:nsmp, D:2 * D][:, None, :], m_n[:nsmp, 0:D][:, None, :])
            nxt_p = (norm_g[layer + 1, 0], m_n[nsmp:nsmp + 1, D:2 * D][:, None, :],
                     m_n[nsmp:nsmp + 1, 0:D][:, None, :])
        else:
            nxt_s = nxt_p = None
        xp, hp = resid_norm(xp, yp, norm_g[layer, 3], mp[5], nxt_p)
        xs, hs = resid_norm(xs, ys, norm_g[layer, 3], ms[5], nxt_s)
    new_ckv = jnp.stack(l_ckv, axis=1)
    new_kpe = jnp.stack(l_kpe, axis=1)
    new_s5_re = jnp.stack(l_s5r, axis=1)
    new_s5_im = jnp.stack(l_s5i, axis=1)
    new_hgrn = jnp.stack(l_hg, axis=1)
    return (xp, xs, new_ckv, new_kpe, new_s5_re, new_s5_im, new_hgrn)
```
